```python
import math
import jax, jax.numpy as jnp
from jax import lax
import numpy as np

D_MODEL = 2048
BATCH = 4
SEQ = 8192
DEPTH = 2

N_A_LAYERS = DEPTH // 2
N_B_LAYERS = DEPTH - N_A_LAYERS
MIX_WIDTH = D_MODEL
MEM_SLOTS = 256
MEM_HEADS = 4
MEM_WIDTH = MIX_WIDTH // 4
MEM_HEAD_DIM = MEM_WIDTH // MEM_HEADS
MLSTM_HEADS = 6
MLSTM_DV = (MIX_WIDTH - MEM_WIDTH) // MLSTM_HEADS
MLSTM_DQK = MLSTM_DV // 2
MLSTM_CHUNK = 128
DIFF_HEADS = 12
DIFF_HD = (MIX_WIDTH - MEM_WIDTH) // DIFF_HEADS // 2
DIFF_DV = 2 * DIFF_HD
Q_BLOCK = 128
ROPE_THETA = 10000.0
N_GROUPS = 4
EXPERTS_PER_GROUP = 8
N_EXPERTS = N_GROUPS * EXPERTS_PER_GROUP
TOP_K = 2
D_FF_EXPERT = 1024
MOE_BLOCK = 128
EPS = 1e-6

A_QK = MLSTM_HEADS * MLSTM_DQK
A_V = MLSTM_HEADS * MLSTM_DV
A_IN = 2 * A_QK + 2 * A_V + 2 * MLSTM_HEADS + MEM_WIDTH
A_SPLITS = [A_QK, 2 * A_QK, 2 * A_QK + A_V, 2 * A_QK + 2 * A_V, 2 * A_QK + 2 * A_V + 2 * MLSTM_HEADS]
DIFF_QK = 2 * DIFF_HEADS * DIFF_HD
B_IN = DIFF_QK + MEM_WIDTH
KV_WIDTH = DIFF_QK + DIFF_HEADS * DIFF_DV
A_OUT = A_V + MEM_WIDTH
B_OUT = DIFF_HEADS * DIFF_DV + MEM_WIDTH

kernel_name = "yoco_mlstm_diffattn_hmoe"


def rms_norm(x, w):
    xf = x.astype(jnp.float32)
    y = xf * lax.rsqrt(jnp.mean(xf * xf, axis=-1, keepdims=True) + EPS)
    return (y * w.astype(jnp.float32)).astype(x.dtype)


def rope(t, positions):
    half = t.shape[-1] // 2
    inv_freq = ROPE_THETA ** (-jnp.arange(half, dtype=jnp.float32) / half)
    ang = positions.astype(jnp.float32)[..., None] * inv_freq
    ang = ang.reshape(ang.shape[:2] + (1,) * (t.ndim - 3) + (half,))
    cos, sin = jnp.cos(ang), jnp.sin(ang)
    t1 = t[..., :half].astype(jnp.float32)
    t2 = t[..., half:].astype(jnp.float32)
    return jnp.concatenate([t1 * cos - t2 * sin, t1 * sin + t2 * cos], axis=-1).astype(t.dtype)


def mlstm_chunkwise(q, k, v, i_pre, f_pre):
    B, H, S, DQK = q.shape
    DV = v.shape[-1]
    L = MLSTM_CHUNK
    NC = S // L
    f32 = jnp.float32
    q = q.astype(f32) * (DQK ** -0.5)
    k = k.astype(f32)
    v = v.astype(f32)
    logi = i_pre.astype(f32)
    logf = jax.nn.log_sigmoid(f_pre.astype(f32))

    def to_chunks(t):
        t = t.reshape(t.shape[:2] + (NC, L) + t.shape[3:])
        return jnp.moveaxis(t, 2, 0)

    xs = (to_chunks(q), to_chunks(k), to_chunks(v), to_chunks(logi), to_chunks(logf))
    causal = jnp.tril(jnp.ones((L, L), dtype=bool))

    def step(carry, inp):
        C, n, m = carry
        qj, kj, vj, ij, fj = inp
        b = jnp.cumsum(fj, axis=-1)
        d_log = jnp.where(causal, b[..., :, None] - b[..., None, :] + ij[..., None, :], -jnp.inf)
        inter = b + m[..., None]
        m_t = jnp.maximum(inter, jnp.max(d_log, axis=-1))
        w_intra = jnp.exp(d_log - m_t[..., None])
        w_inter = jnp.exp(inter - m_t)
        s = jnp.einsum('bhtd,bhsd->bhts', qj, kj) * w_intra
        num = jnp.einsum('bhts,bhsv->bhtv', s, vj) + w_inter[..., None] * jnp.einsum('bhtd,bhdv->bhtv', qj, C)
        den = jnp.sum(s, axis=-1) + w_inter * jnp.einsum('bhtd,bhd->bht', qj, n)
        h = num / jnp.maximum(jnp.abs(den), jnp.exp(-m_t))[..., None]
        b_last = b[..., -1]
        st_log = b_last[..., None] - b + ij
        m_new = jnp.maximum(b_last + m, jnp.max(st_log, axis=-1))
        w_st = jnp.exp(st_log - m_new[..., None])
        decay = jnp.exp(b_last + m - m_new)
        C_new = decay[..., None, None] * C + jnp.einsum('bhsd,bhsv->bhdv', kj * w_st[..., None], vj)
        n_new = decay[..., None] * n + jnp.einsum('bhs,bhsd->bhd', w_st, kj)
        return (C_new, n_new, m_new), h

    init = (jnp.zeros((B, H, DQK, DV), f32), jnp.zeros((B, H, DQK), f32), jnp.zeros((B, H), f32))
    _, hs = lax.scan(step, init, xs)
    return jnp.moveaxis(hs, 0, 2).reshape(B, H, S, DV)


def mem_kv(mem, norm_w, w_kv, knorm):
    B, M, _ = mem.shape
    kv = rms_norm(mem, norm_w) @ w_kv
    k, v = jnp.split(kv, [MEM_WIDTH], axis=-1)
    k = rms_norm(k.reshape(B, M, MEM_HEADS, MEM_HEAD_DIM), knorm)
    return k, v.reshape(B, M, MEM_HEADS, MEM_HEAD_DIM)


def mem_attention(mq, mem_k, mem_v, qnorm):
    B, S, _ = mq.shape
    q = rms_norm(mq.reshape(B, S, MEM_HEADS, MEM_HEAD_DIM), qnorm)
    s = jnp.einsum('bshd,bmhd->bhsm', q, mem_k, preferred_element_type=jnp.float32) * (MEM_HEAD_DIM ** -0.5)
    p = jax.nn.softmax(s, axis=-1).astype(mem_v.dtype)
    return jnp.einsum('bhsm,bmhd->bshd', p, mem_v).reshape(B, S, MEM_WIDTH)


def a_mixer(h, mk, mv, w_in, gate_bias, head_norm, mem_qnorm, w_out):
    B, S, _ = h.shape
    q, k, v, o, g, mq = jnp.split(h @ w_in, A_SPLITS, axis=-1)
    g = g.astype(jnp.float32) + gate_bias.astype(jnp.float32)
    i_pre = g[..., :MLSTM_HEADS].transpose(0, 2, 1)
    f_pre = g[..., MLSTM_HEADS:].transpose(0, 2, 1)

    def heads(t, d):
        return t.reshape(B, S, MLSTM_HEADS, d).transpose(0, 2, 1, 3)

    hm = mlstm_chunkwise(heads(q, MLSTM_DQK), heads(k, MLSTM_DQK), heads(v, MLSTM_DV), i_pre, f_pre)
    hm = hm.transpose(0, 2, 1, 3).astype(h.dtype)
    hm = rms_norm(hm, head_norm) * jax.nn.sigmoid(o).reshape(B, S, MLSTM_HEADS, MLSTM_DV)
    mo = mem_attention(mq, mk, mv, mem_qnorm)
    return jnp.concatenate([hm.reshape(B, S, A_V), mo], axis=-1) @ w_out


def shared_kv(x, positions, kv_norm, kv_w, kv_knorm):
    B, S, _ = x.shape
    k, v = jnp.split(rms_norm(x, kv_norm) @ kv_w, [DIFF_QK], axis=-1)
    k = rope(rms_norm(k.reshape(B, S, DIFF_HEADS, 2, DIFF_HD), kv_knorm), positions)
    return k, v.reshape(B, S, DIFF_HEADS, DIFF_DV)


def diff_attention(q, k, v, lam):
    B, S = q.shape[:2]
    nb = S // Q_BLOCK
    qb = jnp.moveaxis(q.reshape((B, nb, Q_BLOCK) + q.shape[2:]), 1, 0)
    kpos = jnp.arange(S)
    scale = DIFF_HD ** -0.5

    def one_block(args):
        qblk, blk = args
        s = jnp.einsum('bqhcd,bkhcd->bhcqk', qblk, k, preferred_element_type=jnp.float32) * scale
        qpos = blk * Q_BLOCK + jnp.arange(Q_BLOCK)
        s = jnp.where(kpos[None, :] <= qpos[:, None], s, -jnp.inf)
        p = jax.nn.softmax(s, axis=-1)
        a = (p[:, :, 0] - lam * p[:, :, 1]).astype(v.dtype)
        return jnp.einsum('bhqk,bkhv->bqhv', a, v)

    out = lax.map(one_block, (qb, jnp.arange(nb)))
    return jnp.moveaxis(out, 0, 1).reshape(B, S, DIFF_HEADS, DIFF_DV)


def b_mixer(h, positions, k_sh, v_sh, mk, mv, w_in, qnorm, lam_vecs, subln, lambda_init, mem_qnorm, w_out):
    B, S, _ = h.shape
    q, mq = jnp.split(h @ w_in, [DIFF_QK], axis=-1)
    q = rope(rms_norm(q.reshape(B, S, DIFF_HEADS, 2, DIFF_HD), qnorm), positions)
    lv = lam_vecs.astype(jnp.float32)
    lam = jnp.exp(jnp.sum(lv[0] * lv[1])) - jnp.exp(jnp.sum(lv[2] * lv[3])) + lambda_init
    o = diff_attention(q, k_sh, v_sh, lam)
    o = rms_norm(o, subln) * (1.0 - lambda_init)
    mo = mem_attention(mq, mk, mv, mem_qnorm)
    return jnp.concatenate([o.reshape(B, S, DIFF_HEADS * DIFF_DV), mo], axis=-1) @ w_out


def hier_moe(h, w_group, b_group, w_expert, b_expert, w_gate, w_up, w_down):
    B, S, D = h.shape
    T = B * S
    xt = h.reshape(T, D)
    g_prob = jax.nn.softmax(jnp.dot(xt, w_group, preferred_element_type=jnp.float32) + b_group, axis=-1)
    g_w, g_idx = lax.top_k(g_prob, 1)
    e_logits = (jnp.dot(xt, w_expert, preferred_element_type=jnp.float32) + b_expert).reshape(T, N_GROUPS, EXPERTS_PER_GROUP)
    e_logits = e_logits[jnp.arange(T), g_idx[:, 0]]
    e_w, e_idx = lax.top_k(jax.nn.softmax(e_logits, axis=-1), TOP_K)
    e_w = e_w / jnp.sum(e_w, axis=-1, keepdims=True)
    weights = (g_w * e_w).reshape(-1)
    eid = (g_idx * EXPERTS_PER_GROUP + e_idx).reshape(-1)
    A = T * TOP_K
    tok = jnp.repeat(jnp.arange(T, dtype=jnp.int32), TOP_K)
    order = jnp.argsort(eid)
    eid_s = eid[order]
    counts = jnp.bincount(eid, length=N_EXPERTS)
    padded = (counts + MOE_BLOCK - 1) // MOE_BLOCK * MOE_BLOCK
    start = jnp.cumsum(counts) - counts
    pend = jnp.cumsum(padded)
    pstart = pend - padded
    dest = pstart[eid_s] + jnp.arange(A) - start[eid_s]
    n_blocks = -(-A // MOE_BLOCK) + N_EXPERTS
    P = n_blocks * MOE_BLOCK
    tok_buf = jnp.full((P,), T, jnp.int32).at[dest].set(tok[order])
    wt_buf = jnp.zeros((P,), jnp.float32).at[dest].set(weights[order])
    blk_exp = jnp.minimum(jnp.searchsorted(pend, jnp.arange(n_blocks) * MOE_BLOCK, side='right'), N_EXPERTS - 1)
    x_pad = jnp.concatenate([xt, jnp.zeros((1, D), xt.dtype)], axis=0)

    def expert_block(args):
        idx, wt, e = args
        xb = x_pad[idx]
        a = jax.nn.silu(xb @ w_gate[e]) * (xb @ w_up[e])
        return ((a @ w_down[e]).astype(jnp.float32) * wt[:, None]).astype(xt.dtype)

    yb = lax.map(expert_block, (tok_buf.reshape(n_blocks, MOE_BLOCK), wt_buf.reshape(n_blocks, MOE_BLOCK), blk_exp))
    dest_unsorted = jnp.zeros((A,), dest.dtype).at[order].set(dest)
    out = yb.reshape(P, D)[dest_unsorted].reshape(T, TOP_K, D).sum(axis=1)
    return out.reshape(B, S, D)


def setup_inputs(seed: int = 0) -> dict:
    key = jax.random.key(seed)
    ks = iter(jax.random.split(key, 40))
    f32 = jnp.float32

    def nrm(shape, scale):
        return scale * jax.random.normal(next(ks), shape, f32)

    def gain(shape):
        return 1.0 + 0.02 * jax.random.normal(next(ks), shape, f32)

    x = nrm((BATCH, SEQ, D_MODEL), 1.0)
    mem = nrm((BATCH, MEM_SLOTS, D_MODEL), 1.0)
    offset = jax.random.randint(next(ks), (BATCH, 1), 0, 1024, dtype=jnp.int32)
    positions = offset + jnp.arange(SEQ, dtype=jnp.int32)[None, :]
    gate_bias = jnp.concatenate([
        nrm((N_A_LAYERS, MLSTM_HEADS), 0.1),
        jnp.linspace(3.0, 6.0, MLSTM_HEADS, dtype=f32)[None, :] + nrm((N_A_LAYERS, MLSTM_HEADS), 0.01)], axis=-1)
    return {
        "x": x,
        "mem": mem,
        "positions": positions,
        "attn_norm": gain((DEPTH, D_MODEL)),
        "mem_norm": gain((DEPTH, D_MODEL)),
        "mem_w_kv": nrm((DEPTH, D_MODEL, 2 * MEM_WIDTH), D_MODEL ** -0.5),
        "mem_qnorm": gain((DEPTH, MEM_HEAD_DIM)),
        "mem_knorm": gain((DEPTH, MEM_HEAD_DIM)),
        "a_w_in": nrm((N_A_LAYERS, D_MODEL, A_IN), D_MODEL ** -0.5),
        "a_gate_bias": gate_bias,
        "a_head_norm": gain((N_A_LAYERS, MLSTM_HEADS, MLSTM_DV)),
        "a_w_out": nrm((N_A_LAYERS, A_OUT, D_MODEL), A_OUT ** -0.5),
        "kv_norm": gain((D_MODEL,)),
        "kv_w": nrm((D_MODEL, KV_WIDTH), D_MODEL ** -0.5),
        "kv_knorm": gain((DIFF_HD,)),
        "b_w_in": nrm((N_B_LAYERS, D_MODEL, B_IN), D_MODEL ** -0.5),
        "b_qnorm": gain((N_B_LAYERS, DIFF_HD)),
        "b_lambda": nrm((N_B_LAYERS, 4, DIFF_HD), 0.1),
        "b_subln": gain((N_B_LAYERS, DIFF_DV)),
        "b_w_out": nrm((N_B_LAYERS, B_OUT, D_MODEL), B_OUT ** -0.5),
        "ffn_norm": gain((DEPTH, D_MODEL)),
        "moe_w_group": nrm((DEPTH, D_MODEL, N_GROUPS), D_MODEL ** -0.5),
        "moe_b_group": nrm((DEPTH, N_GROUPS), 0.01),
        "moe_w_expert": nrm((DEPTH, D_MODEL, N_EXPERTS), D_MODEL ** -0.5),
        "moe_b_expert": nrm((DEPTH, N_EXPERTS), 0.01),
        "moe_w_gate": nrm((DEPTH, N_EXPERTS, D_MODEL, D_FF_EXPERT), D_MODEL ** -0.5),
        "moe_w_up": nrm((DEPTH, N_EXPERTS, D_MODEL, D_FF_EXPERT), D_MODEL ** -0.5),
        "moe_w_down": nrm((DEPTH, N_EXPERTS, D_FF_EXPERT, D_MODEL), D_FF_EXPERT ** -0.5),
    }


def reference(x, mem, positions, attn_norm, mem_norm, mem_w_kv, mem_qnorm, mem_knorm,
              a_w_in, a_gate_bias, a_head_norm, a_w_out, kv_norm, kv_w, kv_knorm,
              b_w_in, b_qnorm, b_lambda, b_subln, b_w_out, ffn_norm,
              moe_w_group, moe_b_group, moe_w_expert, moe_b_expert, moe_w_gate, moe_w_up, moe_w_down):
    k_sh = None
    v_sh = None
    for l in range(DEPTH):
        h = rms_norm(x, attn_norm[l])
        mk, mv = mem_kv(mem, mem_norm[l], mem_w_kv[l], mem_knorm[l])
        if l < N_A_LAYERS:
            x = x + a_mixer(h, mk, mv, a_w_in[l], a_gate_bias[l], a_head_norm[l], mem_qnorm[l], a_w_out[l])
        else:
            j = l - N_A_LAYERS
            lambda_init = 0.8 - 0.6 * math.exp(-0.3 * l)
            x = x + b_mixer(h, positions, k_sh, v_sh, mk, mv, b_w_in[j], b_qnorm[j], b_lambda[j],
                            b_subln[j], lambda_init, mem_qnorm[l], b_w_out[j])
        x = x + hier_moe(rms_norm(x, ffn_norm[l]), moe_w_group[l], moe_b_group[l], moe_w_expert[l],
                         moe_b_expert[l], moe_w_gate[l], moe_w_up[l], moe_w_down[l])
        if l == N_A_LAYERS - 1:
            k_sh, v_sh = shared_kv(x, positions, kv_norm, kv_w, kv_knorm)
    return x
```

```python
import functools
import math

import jax
import jax.numpy as jnp
from jax import lax
from jax.experimental import pallas as pl
from jax.experimental.pallas import tpu as pltpu

F32 = jnp.float32
BF16 = jnp.bfloat16
EPS = 1e-6
LANES = 128
VMEM_LIMIT_BYTES = 56 * 1024 * 1024

MLSTM_HEADS = 6
MLSTM_DQK = 128
MLSTM_DV = 256
MLSTM_CHUNK = 128
MEM_HEADS = 4
MEM_HEAD_DIM = 128
DIFF_HEADS = 12
DIFF_HD = 64
DIFF_DV = 128
ROPE_THETA = 10000.0
N_GROUPS = 4
EXPERTS_PER_GROUP = 8
N_EXPERTS = N_GROUPS * EXPERTS_PER_GROUP
TOP_K = 2
MOE_BLOCK = 256


def _params(*semantics):
    return pltpu.CompilerParams(dimension_semantics=semantics, vmem_limit_bytes=VMEM_LIMIT_BYTES)


def _norm_proj_kernel(*refs, n_parts, write_sum, norm_bounds, rope_ranges, has_aux, hd):
    it = iter(refs)
    parts = [next(it) for _ in range(n_parts)]
    normw_ref = next(it)
    w_ref = next(it)
    if rope_ranges:
        cos_ref, sin_ref, ew_ref = next(it), next(it), next(it)
    if has_aux:
        waux_ref = next(it)
    out_ref = next(it)
    if write_sum:
        xsum_ref = next(it)
    if has_aux:
        aux_ref = next(it)
    h_ref = next(it)

    j = pl.program_id(1)
    n_norm = len(norm_bounds) + 1

    @pl.when(j == 0)
    def _():
        x = parts[0][...]
        for p in parts[1:]:
            x = x + p[...]
        if write_sum:
            xsum_ref[...] = x
        xn = x * lax.rsqrt(jnp.mean(x * x, axis=-1, keepdims=True) + EPS)
        for n in range(n_norm):
            h_ref[n] = (xn * normw_ref[n:n + 1, :]).astype(BF16)
        if has_aux:
            aux_ref[...] = jnp.dot(h_ref[0], waux_ref[...], preferred_element_type=F32)

    sel = jnp.int32(0)
    for b in norm_bounds:
        sel = sel + (j >= b).astype(jnp.int32)
    acc = jnp.dot(h_ref[sel], w_ref[...], preferred_element_type=F32)

    if not rope_ranges:
        out_ref[...] = acc.astype(out_ref.dtype)
        return

    is_rope = jnp.bool_(False)
    for lo, hi in rope_ranges:
        is_rope = is_rope | ((j >= lo) & (j < hi))

    @pl.when(is_rope)
    def _():
        ew = ew_ref[j]
        cs = cos_ref[...]
        sn = sin_ref[...]
        r = lax.broadcasted_iota(jnp.int32, (2 * LANES, LANES), 0)
        c = lax.broadcasted_iota(jnp.int32, (2 * LANES, LANES), 1)
        seg = ((r % LANES) // hd == c // hd).astype(BF16)
        lane = lax.broadcasted_iota(jnp.int32, (acc.shape[0], LANES), 1)
        first_half = (lane % hd) < (hd // 2)
        for s in range(acc.shape[1] // LANES):
            t = acc[:, s * LANES:(s + 1) * LANES]
            tt = t * t
            hi_part = tt.astype(BF16)
            lo_part = (tt - hi_part.astype(F32)).astype(BF16)
            ss = jnp.dot(jnp.concatenate([hi_part, lo_part], axis=1), seg, preferred_element_type=F32)
            y = t * lax.rsqrt(ss * (1.0 / hd) + EPS) * ew
            rot = jnp.where(first_half, pltpu.roll(y, LANES - hd // 2, 1), pltpu.roll(y, hd // 2, 1))
            out_ref[:, s * LANES:(s + 1) * LANES] = (y * cs + rot * sn).astype(out_ref.dtype)

    @pl.when(jnp.logical_not(is_rope))
    def _():
        out_ref[...] = acc.astype(out_ref.dtype)


def norm_proj(x_parts, part_blocks, n_rows, norm_w, w, *, tm, tn, norm_bounds=(), rope=None,
              rope_ranges=(), w_aux=None, write_sum=False):
    d = w.shape[0]
    n = w.shape[1]
    n_norm = norm_w.shape[0]
    grid = (n_rows // tm, n // tn)
    in_specs = [pl.BlockSpec((tm, d), functools.partial(lambda i, j, o: (i + o, 0), o=o)) for o in part_blocks]
    args = list(x_parts)
    in_specs += [pl.BlockSpec((n_norm, d), lambda i, j: (0, 0)), pl.BlockSpec((d, tn), lambda i, j: (0, j))]
    args += [norm_w, w]
    if rope_ranges:
        cos, sin, ew = rope
        in_specs += [pl.BlockSpec((tm, LANES), lambda i, j: (i, 0)), pl.BlockSpec((tm, LANES), lambda i, j: (i, 0)),
                     pl.BlockSpec(ew.shape, lambda i, j: (0, 0, 0))]
        args += [cos, sin, ew]
    if w_aux is not None:
        in_specs.append(pl.BlockSpec((d, LANES), lambda i, j: (0, 0)))
        args.append(w_aux)
    out_shape = [jax.ShapeDtypeStruct((n_rows, n), BF16)]
    out_specs = [pl.BlockSpec((tm, tn), lambda i, j: (i, j))]
    if write_sum:
        out_shape.append(jax.ShapeDtypeStruct((n_rows, d), F32))
        out_specs.append(pl.BlockSpec((tm, d), lambda i, j: (i, 0)))
    if w_aux is not None:
        out_shape.append(jax.ShapeDtypeStruct((n_rows, LANES), F32))
        out_specs.append(pl.BlockSpec((tm, LANES), lambda i, j: (i, 0)))
    kern = functools.partial(_norm_proj_kernel, n_parts=len(x_parts), write_sum=write_sum,
                             norm_bounds=tuple(norm_bounds), rope_ranges=tuple(rope_ranges),
                             has_aux=w_aux is not None, hd=DIFF_HD)
    return pl.pallas_call(
        kern, grid=grid, in_specs=in_specs, out_specs=out_specs, out_shape=out_shape,
        scratch_shapes=[pltpu.VMEM((n_norm, tm, d), BF16)],
        compiler_params=_params("parallel", "arbitrary"), name="norm_proj")(*args)


def _log_sigmoid(x):
    return jnp.minimum(x, 0.0) - jnp.log1p(jnp.exp(-jnp.abs(x)))


def _mlstm_kernel(q_ref, k_ref, v_ref, o_ref, g_ref, bias_ref, hnorm_ref, out_ref, c_ref, n_ref, m_ref):
    L, H, DQK, DV = MLSTM_CHUNK, MLSTM_HEADS, MLSTM_DQK, MLSTM_DV
    scale = DQK ** -0.5

    @pl.when(pl.program_id(1) == 0)
    def _():
        c_ref[...] = jnp.zeros_like(c_ref)
        n_ref[...] = jnp.zeros_like(n_ref)
        m_ref[...] = jnp.zeros_like(m_ref)

    g = g_ref[...] + bias_ref[...]
    ls = _log_sigmoid(g)
    g_t = g.T
    ls_t = ls.T
    row = lax.broadcasted_iota(jnp.int32, (L, L), 0)
    col = lax.broadcasted_iota(jnp.int32, (L, L), 1)
    causal = col <= row

    for h in range(H):
        q = q_ref[:, h * DQK:(h + 1) * DQK]
        k = k_ref[:, h * DQK:(h + 1) * DQK]
        v = v_ref[:, h * DV:(h + 1) * DV]
        i_col = g[:, h:h + 1]
        i_row = g_t[h:h + 1, :]
        lf_col = ls[:, H + h:H + h + 1]
        lf_row = ls_t[H + h:H + h + 1, :]
        b_col = jnp.sum(jnp.where(causal, lf_row, 0.0), axis=1, keepdims=True)
        b_row = jnp.sum(jnp.where(row <= col, lf_col, 0.0), axis=0, keepdims=True)
        m_prev = m_ref[h][:, 0:1]
        c_prev = c_ref[h]
        n_prev = n_ref[h]

        d_log = jnp.where(causal, b_col - b_row + i_row, -jnp.inf)
        inter = b_col + m_prev
        m_t = jnp.maximum(inter, jnp.max(d_log, axis=1, keepdims=True))
        w_intra = jnp.exp(d_log - m_t)
        w_inter = jnp.exp(inter - m_t) * scale
        qk = lax.dot_general(q, k, (((1,), (1,)), ((), ())), preferred_element_type=F32)
        s = qk * scale * w_intra
        num = jnp.dot(s.astype(BF16), v, preferred_element_type=F32)
        num = num + w_inter * jnp.dot(q, c_prev.astype(BF16), preferred_element_type=F32)
        den = jnp.sum(s, axis=1, keepdims=True) + w_inter * jnp.sum(q.astype(F32) * n_prev, axis=1, keepdims=True)
        hout = num / jnp.maximum(jnp.abs(den), jnp.exp(-m_t))

        b_last = b_row[:, L - 1:L]
        st_row = b_last - b_row + i_row
        m_new = jnp.maximum(b_last + m_prev, jnp.max(st_row, axis=1, keepdims=True))
        w_st_col = jnp.exp(b_last - b_col + i_col - m_new)
        decay = jnp.exp(b_last + m_prev - m_new)
        kw = k.astype(F32) * w_st_col
        c_ref[h] = decay * c_prev + lax.dot_general(kw.astype(BF16), v, (((0,), (0,)), ((), ())),
                                                    preferred_element_type=F32)
        n_ref[h] = decay * n_prev + jnp.sum(kw, axis=0, keepdims=True)
        m_ref[h] = jnp.broadcast_to(m_new, (1, LANES))

        hn = hout * lax.rsqrt(jnp.mean(hout * hout, axis=-1, keepdims=True) + EPS) * hnorm_ref[h:h + 1, :]
        og = o_ref[:, h * DV:(h + 1) * DV].astype(F32)
        out_ref[:, h * DV:(h + 1) * DV] = (hn * jax.nn.sigmoid(og)).astype(out_ref.dtype)


def mlstm(proj, gates, gate_bias, head_norm, batch, seq):
    L, H, DQK, DV = MLSTM_CHUNK, MLSTM_HEADS, MLSTM_DQK, MLSTM_DV
    nc = seq // L
    qk_w, v_w = H * DQK, H * DV
    row = lambda b, c: b * nc + c
    return pl.pallas_call(
        _mlstm_kernel, grid=(batch, nc),
        in_specs=[pl.BlockSpec((L, qk_w), lambda b, c: (row(b, c), 0)),
                  pl.BlockSpec((L, qk_w), lambda b, c: (row(b, c), 1)),
                  pl.BlockSpec((L, v_w), lambda b, c: (row(b, c), 1)),
                  pl.BlockSpec((L, v_w), lambda b, c: (row(b, c), 2)),
                  pl.BlockSpec((L, LANES), lambda b, c: (row(b, c), 0)),
                  pl.BlockSpec((1, LANES), lambda b, c: (0, 0)),
                  pl.BlockSpec((H, DV), lambda b, c: (0, 0))],
        out_specs=pl.BlockSpec((L, v_w), lambda b, c: (row(b, c), 0)),
        out_shape=jax.ShapeDtypeStruct((batch * seq, v_w), BF16),
        scratch_shapes=[pltpu.VMEM((H, DQK, DV), F32), pltpu.VMEM((H, 1, DQK), F32), pltpu.VMEM((H, 1, LANES), F32)],
        compiler_params=_params("parallel", "arbitrary"), name="mlstm")(
            proj, proj, proj, proj, gates, gate_bias, head_norm)


def _mem_attn_kernel(q_ref, k_ref, v_ref, qn_ref, kn_ref, out_ref):
    hd = MEM_HEAD_DIM
    scale = hd ** -0.5
    for h in range(MEM_HEADS):
        sl = slice(h * hd, (h + 1) * hd)
        q = q_ref[:, sl].astype(F32)
        q = q * lax.rsqrt(jnp.mean(q * q, axis=-1, keepdims=True) + EPS) * qn_ref[...]
        k = k_ref[:, sl].astype(F32)
        k = k * lax.rsqrt(jnp.mean(k * k, axis=-1, keepdims=True) + EPS) * kn_ref[...]
        s = lax.dot_general(q.astype(BF16), k.astype(BF16), (((1,), (1,)), ((), ())),
                            preferred_element_type=F32) * scale
        p = jnp.exp(s - jnp.max(s, axis=-1, keepdims=True))
        p = p / jnp.sum(p, axis=-1, keepdims=True)
        out_ref[:, sl] = jnp.dot(p.astype(BF16), v_ref[:, sl], preferred_element_type=F32).astype(out_ref.dtype)


def mem_attn(proj, q_col_block, mkv, layer, qnorm, knorm, batch, seq, slots, tm):
    width = MEM_HEADS * MEM_HEAD_DIM
    nt = seq // tm
    return pl.pallas_call(
        _mem_attn_kernel, grid=(batch, nt),
        in_specs=[pl.BlockSpec((tm, width), lambda b, i: (b * nt + i, q_col_block)),
                  pl.BlockSpec((slots, width), lambda b, i: (b, 2 * layer)),
                  pl.BlockSpec((slots, width), lambda b, i: (b, 2 * layer + 1)),
                  pl.BlockSpec((1, MEM_HEAD_DIM), lambda b, i: (0, 0)),
                  pl.BlockSpec((1, MEM_HEAD_DIM), lambda b, i: (0, 0))],
        out_specs=pl.BlockSpec((tm, width), lambda b, i: (b * nt + i, 0)),
        out_shape=jax.ShapeDtypeStruct((batch * seq, width), BF16),
        compiler_params=_params("parallel", "parallel"), name="mem_attn")(proj, mkv, mkv, qnorm, knorm)


def _out_proj_kernel(a_ref, m_ref, wa_ref, wm_ref, x_ref, fn_ref, wr_ref, br_ref, xo_ref, route_ref):
    y = jnp.dot(a_ref[...], wa_ref[...], preferred_element_type=F32)
    y = y + jnp.dot(m_ref[...], wm_ref[...], preferred_element_type=F32)
    x = x_ref[...] + y
    xo_ref[...] = x
    hn = (x * lax.rsqrt(jnp.mean(x * x, axis=-1, keepdims=True) + EPS) * fn_ref[...]).astype(BF16)
    lg = jnp.dot(hn, wr_ref[...], preferred_element_type=F32) + br_ref[...]
    lane = lax.broadcasted_iota(jnp.int32, lg.shape, 1).astype(F32)
    neg = -jnp.inf
    gl = jnp.where(lane < N_GROUPS, lg, neg)
    gmax = jnp.max(gl, axis=1, keepdims=True)
    g_w = 1.0 / jnp.sum(jnp.exp(gl - gmax), axis=1, keepdims=True)
    g_idx = jnp.min(jnp.where(gl == gmax, lane, float(LANES)), axis=1, keepdims=True)
    e_lo = N_GROUPS + EXPERTS_PER_GROUP * g_idx
    el = jnp.where((lane >= e_lo) & (lane < e_lo + EXPERTS_PER_GROUP), lg, neg)
    max1 = jnp.max(el, axis=1, keepdims=True)
    e1 = jnp.min(jnp.where(el == max1, lane, float(LANES)), axis=1, keepdims=True)
    el2 = jnp.where(lane == e1, neg, el)
    max2 = jnp.max(el2, axis=1, keepdims=True)
    e2 = jnp.min(jnp.where(el2 == max2, lane, float(LANES)), axis=1, keepdims=True)
    p2 = jnp.exp(max2 - max1)
    w1 = g_w / (1.0 + p2)
    w2 = g_w * p2 / (1.0 + p2)
    route = jnp.where(lane == 0, w1, jnp.where(lane == 1, w2, jnp.where(
        lane == 2, e1 - N_GROUPS, jnp.where(lane == 3, e2 - N_GROUPS, 0.0))))
    route_ref[...] = route


def out_proj(a, m, w_a, w_m, x, ffn_norm, w_router, b_router, tm):
    t, d = x.shape
    ka, km = a.shape[1], m.shape[1]
    return pl.pallas_call(
        _out_proj_kernel, grid=(t // tm,),
        in_specs=[pl.BlockSpec((tm, ka), lambda i: (i, 0)), pl.BlockSpec((tm, km), lambda i: (i, 0)),
                  pl.BlockSpec((ka, d), lambda i: (0, 0)), pl.BlockSpec((km, d), lambda i: (0, 0)),
                  pl.BlockSpec((tm, d), lambda i: (i, 0)), pl.BlockSpec((1, d), lambda i: (0, 0)),
                  pl.BlockSpec((d, LANES), lambda i: (0, 0)), pl.BlockSpec((1, LANES), lambda i: (0, 0))],
        out_specs=[pl.BlockSpec((tm, d), lambda i: (i, 0)), pl.BlockSpec((tm, LANES), lambda i: (i, 0))],
        out_shape=[jax.ShapeDtypeStruct((t, d), F32), jax.ShapeDtypeStruct((t, LANES), F32)],
        compiler_params=_params("parallel"), name="out_proj")(a, m, w_a, w_m, x, ffn_norm, w_router, b_router)


def _moe_kernel(blk_exp_ref, tok_ref, dst_ref, nvalid_ref, nused_ref,
                x_hbm, fn_ref, wt_ref, wg_ref, wu_ref, wd_ref, out_hbm, xbuf, ybuf, gsem, ssem):
    del blk_exp_ref
    bm = MOE_BLOCK
    i = pl.program_id(0)
    n_used = nused_ref[0]

    def row_gather(blk, slot, r):
        return pltpu.make_async_copy(x_hbm.at[pl.ds(tok_ref[blk * bm + r], 1)], xbuf.at[slot, pl.ds(r, 1)],
                                     gsem.at[slot])

    def row_scatter(blk, r):
        return pltpu.make_async_copy(ybuf.at[pl.ds(r, 1)], out_hbm.at[pl.ds(dst_ref[blk * bm + r], 1)], ssem.at[0])

    def start_gather(blk, slot):
        lax.fori_loop(0, bm, lambda r, c: (row_gather(blk, slot, r).start(), c)[1], 0)

    def wait_gather(blk, slot):
        lax.fori_loop(0, bm, lambda r, c: (row_gather(blk, slot, r).wait(), c)[1], 0)

    def start_scatter(blk):
        lax.fori_loop(0, nvalid_ref[blk], lambda r, c: (row_scatter(blk, r).start(), c)[1], 0)

    def wait_scatter(blk):
        lax.fori_loop(0, nvalid_ref[blk], lambda r, c: (row_scatter(blk, r).wait(), c)[1], 0)

    @pl.when(i == 0)
    def _():
        start_gather(0, 0)

    @pl.when(i < n_used)
    def _():
        slot = i % 2

        @pl.when(i + 1 < n_used)
        def _():
            start_gather(i + 1, 1 - slot)

        wait_gather(i, slot)
        x = xbuf[slot]
        xn = (x * lax.rsqrt(jnp.mean(x * x, axis=-1, keepdims=True) + EPS) * fn_ref[...]).astype(BF16)
        gate = jnp.dot(xn, wg_ref[...], preferred_element_type=F32)
        up = jnp.dot(xn, wu_ref[...], preferred_element_type=F32)
        act = (gate * jax.nn.sigmoid(gate) * up).astype(BF16)
        y = jnp.dot(act, wd_ref[...], preferred_element_type=F32) * wt_ref[...]

        @pl.when(i > 0)
        def _():
            wait_scatter(i - 1)

        ybuf[...] = y
        start_scatter(i)

        @pl.when(i == n_used - 1)
        def _():
            wait_scatter(i)


def moe_experts(x, ffn_norm, tables, w_gate, w_up, w_down):
    bm = MOE_BLOCK
    t, d = x.shape
    f = w_gate.shape[2]
    tok_buf, dst_buf, wt_buf, blk_exp, n_valid, n_used = tables
    n_blocks = blk_exp.shape[0]
    const = lambda i, *_: (0, 0)
    expert = lambda i, be, *_: (be[i], 0, 0)
    grid_spec = pltpu.PrefetchScalarGridSpec(
        num_scalar_prefetch=5, grid=(n_blocks,),
        in_specs=[pl.BlockSpec(memory_space=pl.ANY),
                  pl.BlockSpec((1, d), const),
                  pl.BlockSpec((bm, 1), lambda i, *_: (i, 0)),
                  pl.BlockSpec((None, d, f), expert),
                  pl.BlockSpec((None, d, f), expert),
                  pl.BlockSpec((None, f, d), expert)],
        out_specs=pl.BlockSpec(memory_space=pl.ANY),
        scratch_shapes=[pltpu.VMEM((2, bm, d), F32), pltpu.VMEM((bm, d), F32),
                        pltpu.SemaphoreType.DMA((2,)), pltpu.SemaphoreType.DMA((1,))])
    return pl.pallas_call(
        _moe_kernel, grid_spec=grid_spec, out_shape=jax.ShapeDtypeStruct((TOP_K * t, d), F32),
        compiler_params=_params("arbitrary"), name="moe_experts")(
            blk_exp, tok_buf, dst_buf, n_valid, n_used, x, ffn_norm, wt_buf, w_gate, w_up, w_down)


def moe_tables(route, n_blocks):
    bm = MOE_BLOCK
    t = route.shape[0]
    a = TOP_K * t
    eid = route[:, 2:4].astype(jnp.int32).T.reshape(a)
    wts = route[:, 0:2].T.reshape(a)
    tok = jnp.tile(jnp.arange(t, dtype=jnp.int32), TOP_K)
    order = jnp.argsort(eid, stable=True).astype(jnp.int32)
    eid_s = eid[order]
    counts = jnp.bincount(eid, length=N_EXPERTS).astype(jnp.int32)
    padded = (counts + bm - 1) // bm * bm
    start = jnp.cumsum(counts) - counts
    pend = jnp.cumsum(padded)
    pstart = pend - padded
    dest = pstart[eid_s] + jnp.arange(a, dtype=jnp.int32) - start[eid_s]
    p = n_blocks * bm
    tok_buf = jnp.zeros((p,), jnp.int32).at[dest].set(tok[order])
    wt_buf = jnp.zeros((p,), F32).at[dest].set(wts[order])
    dst_buf = jnp.zeros((p,), jnp.int32).at[dest].set(order)
    blk_start = jnp.arange(n_blocks, dtype=jnp.int32) * bm
    blk_exp = jnp.minimum(jnp.searchsorted(pend, blk_start, side='right'), N_EXPERTS - 1).astype(jnp.int32)
    n_valid = jnp.clip(counts[blk_exp] - (blk_start - pstart[blk_exp]), 0, bm).astype(jnp.int32)
    n_used = (pend[-1] // bm).astype(jnp.int32).reshape(1)
    return tok_buf, dst_buf, wt_buf.reshape(p, 1), blk_exp, n_valid, n_used


def moe_layer(x_mid, route, ffn_norm, w_gate, w_up, w_down):
    t = x_mid.shape[0]
    a = TOP_K * t
    n_blocks = -(-a // MOE_BLOCK) + N_EXPERTS
    return moe_experts(x_mid, ffn_norm, moe_tables(route, n_blocks),
                       w_gate.astype(BF16), w_up.astype(BF16), w_down.astype(BF16))


def _diff_attn_kernel(q_ref, k_ref, v_ref, lam_ref, subln_ref, out_ref, qs_ref, m_ref, l_ref, acc_ref,
                      *, tq, lambda_init):
    hd = DIFF_HD
    qi = pl.program_id(2)
    q = q_ref[...]
    lane = lax.broadcasted_iota(jnp.int32, q.shape, 1)
    qs_ref[0:tq, :] = jnp.where(lane < hd, q, jnp.zeros_like(q))
    qs_ref[tq:2 * tq, :] = jnp.where(lane >= hd, q, jnp.zeros_like(q))
    m_ref[...] = jnp.full_like(m_ref, -jnp.inf)
    l_ref[...] = jnp.zeros_like(l_ref)
    acc_ref[...] = jnp.zeros_like(acc_ref)

    def tile(ki, masked):
        off = pl.multiple_of(ki * tq, tq)
        k = k_ref[pl.ds(off, tq), :]
        v = v_ref[pl.ds(off, tq), :]
        s = lax.dot_general(qs_ref[...], k, (((1,), (1,)), ((), ())), preferred_element_type=F32)
        if masked:
            r = lax.broadcasted_iota(jnp.int32, s.shape, 0)
            c = lax.broadcasted_iota(jnp.int32, s.shape, 1)
            s = jnp.where(c <= r % tq, s, -jnp.inf)
        m_prev = m_ref[...]
        m_new = jnp.maximum(m_prev, jnp.max(s, axis=1, keepdims=True))
        alpha = jnp.exp(m_prev - m_new)
        p = jnp.exp(s - m_new)
        l_ref[...] = alpha * l_ref[...] + jnp.sum(p, axis=1, keepdims=True)
        acc_ref[...] = alpha * acc_ref[...] + jnp.dot(p.astype(BF16), v, preferred_element_type=F32)
        m_ref[...] = m_new

    lax.fori_loop(0, qi, lambda ki, c: (tile(ki, False), c)[1], 0)
    tile(qi, True)

    lv = lam_ref[...]
    lam = (jnp.exp(jnp.sum(lv[0:1] * lv[1:2], axis=1, keepdims=True))
           - jnp.exp(jnp.sum(lv[2:3] * lv[3:4], axis=1, keepdims=True)) + lambda_init)
    o = acc_ref[0:tq, :] / l_ref[0:tq, :] - lam * (acc_ref[tq:2 * tq, :] / l_ref[tq:2 * tq, :])
    o = o * lax.rsqrt(jnp.mean(o * o, axis=-1, keepdims=True) + EPS) * subln_ref[...] * (1.0 - lambda_init)
    out_ref[...] = o.astype(out_ref.dtype)


def diff_attn(proj, lam_vecs, subln, lambda_init, batch, seq, tq):
    nh = DIFF_HEADS
    nq = seq // tq
    kern = functools.partial(_diff_attn_kernel, tq=tq, lambda_init=lambda_init)
    return pl.pallas_call(
        kern, grid=(batch, nh, nq),
        in_specs=[pl.BlockSpec((tq, LANES), lambda b, h, i: (b * nq + i, 2 * nh + h)),
                  pl.BlockSpec((seq, LANES), lambda b, h, i: (b, h)),
                  pl.BlockSpec((seq, LANES), lambda b, h, i: (b, nh + h)),
                  pl.BlockSpec((4, DIFF_HD), lambda b, h, i: (0, 0)),
                  pl.BlockSpec((1, DIFF_DV), lambda b, h, i: (0, 0))],
        out_specs=pl.BlockSpec((tq, DIFF_DV), lambda b, h, i: (b * nq + i, h)),
        out_shape=jax.ShapeDtypeStruct((batch * seq, nh * DIFF_DV), BF16),
        scratch_shapes=[pltpu.VMEM((2 * tq, LANES), BF16), pltpu.VMEM((2 * tq, 1), F32),
                        pltpu.VMEM((2 * tq, 1), F32), pltpu.VMEM((2 * tq, DIFF_DV), F32)],
        compiler_params=_params("parallel", "parallel", "arbitrary"), name="diff_attn")(
            proj, proj, proj, lam_vecs, subln)


def _combine_kernel(x_ref, a_ref, b_ref, out_ref):
    out_ref[...] = x_ref[...] + a_ref[...] + b_ref[...]


def combine(x_mid, moe_out, tm):
    t, d = x_mid.shape
    nb = t // tm
    return pl.pallas_call(
        _combine_kernel, grid=(nb,),
        in_specs=[pl.BlockSpec((tm, d), lambda i: (i, 0)), pl.BlockSpec((tm, d), lambda i: (i, 0)),
                  pl.BlockSpec((tm, d), lambda i: (i + nb, 0))],
        out_specs=pl.BlockSpec((tm, d), lambda i: (i, 0)),
        out_shape=jax.ShapeDtypeStruct((t, d), F32),
        compiler_params=_params("parallel"), name="combine")(x_mid, moe_out, moe_out)


def _row_tile(n, want):
    while n % want:
        want //= 2
    return want


def _router_weights(w_group, b_group, w_expert, b_expert):
    d = w_group.shape[0]
    pad = LANES - N_GROUPS - N_EXPERTS
    w = jnp.concatenate([w_group, w_expert, jnp.zeros((d, pad), F32)], axis=1).astype(BF16)
    b = jnp.concatenate([b_group, b_expert, jnp.zeros((pad,), F32)]).reshape(1, LANES)
    return w, b


def kernel(x, mem, positions, attn_norm, mem_norm, mem_w_kv, mem_qnorm, mem_knorm, a_w_in, a_gate_bias,
           a_head_norm, a_w_out, kv_norm, kv_w, kv_knorm, b_w_in, b_qnorm, b_lambda, b_subln, b_w_out, ffn_norm,
           moe_w_group, moe_b_group, moe_w_expert, moe_b_expert, moe_w_gate, moe_w_up, moe_w_down):
    batch, seq, d = x.shape
    slots = mem.shape[1]
    t = batch * seq
    tm = _row_tile(t, 512)
    a_qk = MLSTM_HEADS * MLSTM_DQK
    a_v = MLSTM_HEADS * MLSTM_DV
    mem_w = MEM_HEADS * MEM_HEAD_DIM
    diff_qk = 2 * DIFF_HEADS * DIFF_HD
    xt = x.reshape(t, d)

    memt = mem.reshape(batch * slots, d)
    w_mkv = jnp.concatenate([mem_w_kv[0], mem_w_kv[1]], axis=1).astype(BF16)
    (mkv,) = norm_proj([memt], [0], batch * slots, mem_norm, w_mkv, tm=_row_tile(batch * slots, 512), tn=512,
                       norm_bounds=(2 * mem_w // 512,))

    g_lo = 2 * a_qk + 2 * a_v
    w_in = a_w_in[0]
    w_main = jnp.concatenate([w_in[:, :g_lo], w_in[:, g_lo + 2 * MLSTM_HEADS:]], axis=1).astype(BF16)
    w_gates = jnp.pad(w_in[:, g_lo:g_lo + 2 * MLSTM_HEADS], ((0, 0), (0, LANES - 2 * MLSTM_HEADS))).astype(BF16)
    proj0, gates = norm_proj([xt], [0], t, attn_norm[0:1], w_main, tm=tm, tn=512, w_aux=w_gates)
    gate_bias = jnp.pad(a_gate_bias[0], (0, LANES - 2 * MLSTM_HEADS)).reshape(1, LANES)
    hm = mlstm(proj0, gates, gate_bias, a_head_norm[0], batch, seq)
    mo0 = mem_attn(proj0, g_lo // mem_w, mkv, 0, mem_qnorm[0:1], mem_knorm[0:1], batch, seq, slots, _row_tile(seq, 512))
    w_r0, b_r0 = _router_weights(moe_w_group[0], moe_b_group[0], moe_w_expert[0], moe_b_expert[0])
    w_out0 = a_w_out[0].astype(BF16)
    x_mid0, route0 = out_proj(hm, mo0, w_out0[:a_v], w_out0[a_v:], xt, ffn_norm[0:1], w_r0, b_r0, tm)
    moe0 = moe_layer(x_mid0, route0, ffn_norm[0:1], moe_w_gate[0], moe_w_up[0], moe_w_down[0])

    half = DIFF_HD // 2
    inv_freq = ROPE_THETA ** (-jnp.arange(half, dtype=F32) / half)
    ang = positions.astype(F32).reshape(t, 1) * inv_freq[None, :]
    lane = jnp.arange(LANES)
    cos = jnp.cos(ang)[:, lane % half]
    sin = jnp.sin(ang)[:, lane % half] * jnp.where((lane % DIFF_HD) < half, -1.0, 1.0)[None, :]
    tn = 512
    w1 = jnp.concatenate([kv_w, b_w_in[0]], axis=1).astype(BF16)
    n_tiles = w1.shape[1] // tn
    k_tiles = diff_qk // tn
    q_lo = kv_w.shape[1] // tn
    reps = LANES // DIFF_HD
    ew = jnp.zeros((n_tiles, 1, LANES), F32)
    ew = ew.at[0:k_tiles].set(jnp.tile(kv_knorm, reps)[None, None, :])
    ew = ew.at[q_lo:q_lo + k_tiles].set(jnp.tile(b_qnorm[0], reps)[None, None, :] * (DIFF_HD ** -0.5))
    norms1 = jnp.stack([kv_norm, attn_norm[1]])
    nb = t // tm
    proj1, x1 = norm_proj([x_mid0, moe0, moe0], [0, 0, nb], t, norms1, w1, tm=tm, tn=tn, norm_bounds=(q_lo,),
                          rope=(cos, sin, ew), rope_ranges=((0, k_tiles), (q_lo, q_lo + k_tiles)), write_sum=True)

    lambda_init = 0.8 - 0.6 * math.exp(-0.3 * 1)
    oa = diff_attn(proj1, b_lambda[0], b_subln[0:1], lambda_init, batch, seq, _row_tile(seq, 512))
    mo1 = mem_attn(proj1, (kv_w.shape[1] + diff_qk) // mem_w, mkv, 1, mem_qnorm[1:2], mem_knorm[1:2], batch, seq,
                   slots, _row_tile(seq, 512))
    w_r1, b_r1 = _router_weights(moe_w_group[1], moe_b_group[1], moe_w_expert[1], moe_b_expert[1])
    w_out1 = b_w_out[0].astype(BF16)
    n_oa = DIFF_HEADS * DIFF_DV
    x_mid1, route1 = out_proj(oa, mo1, w_out1[:n_oa], w_out1[n_oa:], x1, ffn_norm[1:2], w_r1, b_r1, tm)
    moe1 = moe_layer(x_mid1, route1, ffn_norm[1:2], moe_w_gate[1], moe_w_up[1], moe_w_down[1])
    out = combine(x_mid1, moe1, tm)
    return out.reshape(batch, seq, d)
```

```python
import functools
import math

import jax
import jax.numpy as jnp
from jax import lax
from jax.experimental import pallas as pl
from jax.experimental.pallas import tpu as pltpu

F32 = jnp.float32
BF16 = jnp.bfloat16
EPS = 1e-6
LANES = 128
VMEM_LIMIT_BYTES = 56 * 1024 * 1024

MLSTM_HEADS = 6
MLSTM_DQK = 128
MLSTM_DV = 256
MLSTM_CHUNK = 128
MEM_HEADS = 4
MEM_HEAD_DIM = 128
DIFF_HEADS = 12
DIFF_HD = 64
DIFF_DV = 128
ROPE_THETA = 10000.0
N_GROUPS = 4
EXPERTS_PER_GROUP = 8
N_EXPERTS = N_GROUPS * EXPERTS_PER_GROUP
TOP_K = 2
MOE_BLOCK = 256
MOE_DMA_UNROLL = 8


def _params(*semantics):
    return pltpu.CompilerParams(dimension_semantics=semantics, vmem_limit_bytes=VMEM_LIMIT_BYTES)


def _norm_proj_kernel(*refs, n_parts, write_sum, norm_bounds, rope_ranges, has_aux, hd):
    it = iter(refs)
    parts = [next(it) for _ in range(n_parts)]
    normw_ref = next(it)
    w_ref = next(it)
    if rope_ranges:
        cos_ref, sin_ref, ew_ref = next(it), next(it), next(it)
    if has_aux:
        waux_ref = next(it)
    out_ref = next(it)
    if write_sum:
        xsum_ref = next(it)
    if has_aux:
        aux_ref = next(it)
    h_ref = next(it)

    j = pl.program_id(1)
    n_norm = len(norm_bounds) + 1

    @pl.when(j == 0)
    def _():
        x = parts[0][...]
        for p in parts[1:]:
            x = x + p[...]
        if write_sum:
            xsum_ref[...] = x
        xn = x * lax.rsqrt(jnp.mean(x * x, axis=-1, keepdims=True) + EPS)
        for n in range(n_norm):
            h_ref[n] = (xn * normw_ref[n:n + 1, :]).astype(BF16)
        if has_aux:
            aux_ref[...] = jnp.dot(h_ref[0], waux_ref[...], preferred_element_type=F32)

    sel = jnp.int32(0)
    for b in norm_bounds:
        sel = sel + (j >= b).astype(jnp.int32)
    acc = jnp.dot(h_ref[sel], w_ref[...], preferred_element_type=F32)

    if not rope_ranges:
        out_ref[...] = acc.astype(out_ref.dtype)
        return

    is_rope = jnp.bool_(False)
    for lo, hi in rope_ranges:
        is_rope = is_rope | ((j >= lo) & (j < hi))

    @pl.when(is_rope)
    def _():
        ew = ew_ref[j]
        cs = cos_ref[...]
        sn = sin_ref[...]
        r = lax.broadcasted_iota(jnp.int32, (2 * LANES, LANES), 0)
        c = lax.broadcasted_iota(jnp.int32, (2 * LANES, LANES), 1)
        seg = ((r % LANES) // hd == c // hd).astype(BF16)
        lane = lax.broadcasted_iota(jnp.int32, (acc.shape[0], LANES), 1)
        first_half = (lane % hd) < (hd // 2)
        for s in range(acc.shape[1] // LANES):
            t = acc[:, s * LANES:(s + 1) * LANES]
            tt = t * t
            hi_part = tt.astype(BF16)
            lo_part = (tt - hi_part.astype(F32)).astype(BF16)
            ss = jnp.dot(jnp.concatenate([hi_part, lo_part], axis=1), seg, preferred_element_type=F32)
            y = t * lax.rsqrt(ss * (1.0 / hd) + EPS) * ew
            rot = jnp.where(first_half, pltpu.roll(y, LANES - hd // 2, 1), pltpu.roll(y, hd // 2, 1))
            out_ref[:, s * LANES:(s + 1) * LANES] = (y * cs + rot * sn).astype(out_ref.dtype)

    @pl.when(jnp.logical_not(is_rope))
    def _():
        out_ref[...] = acc.astype(out_ref.dtype)


def norm_proj(x_parts, part_blocks, n_rows, norm_w, w, *, tm, tn, norm_bounds=(), rope=None,
              rope_ranges=(), w_aux=None, write_sum=False):
    d = w.shape[0]
    n = w.shape[1]
    n_norm = norm_w.shape[0]
    grid = (n_rows // tm, n // tn)
    in_specs = [pl.BlockSpec((tm, d), functools.partial(lambda i, j, o: (i + o, 0), o=o)) for o in part_blocks]
    args = list(x_parts)
    in_specs += [pl.BlockSpec((n_norm, d), lambda i, j: (0, 0)), pl.BlockSpec((d, tn), lambda i, j: (0, j))]
    args += [norm_w, w]
    if rope_ranges:
        cos, sin, ew = rope
        in_specs += [pl.BlockSpec((tm, LANES), lambda i, j: (i, 0)), pl.BlockSpec((tm, LANES), lambda i, j: (i, 0)),
                     pl.BlockSpec(ew.shape, lambda i, j: (0, 0, 0))]
        args += [cos, sin, ew]
    if w_aux is not None:
        in_specs.append(pl.BlockSpec((d, LANES), lambda i, j: (0, 0)))
        args.append(w_aux)
    out_shape = [jax.ShapeDtypeStruct((n_rows, n), BF16)]
    out_specs = [pl.BlockSpec((tm, tn), lambda i, j: (i, j))]
    if write_sum:
        out_shape.append(jax.ShapeDtypeStruct((n_rows, d), F32))
        out_specs.append(pl.BlockSpec((tm, d), lambda i, j: (i, 0)))
    if w_aux is not None:
        out_shape.append(jax.ShapeDtypeStruct((n_rows, LANES), F32))
        out_specs.append(pl.BlockSpec((tm, LANES), lambda i, j: (i, 0)))
    kern = functools.partial(_norm_proj_kernel, n_parts=len(x_parts), write_sum=write_sum,
                             norm_bounds=tuple(norm_bounds), rope_ranges=tuple(rope_ranges),
                             has_aux=w_aux is not None, hd=DIFF_HD)
    return pl.pallas_call(
        kern, grid=grid, in_specs=in_specs, out_specs=out_specs, out_shape=out_shape,
        scratch_shapes=[pltpu.VMEM((n_norm, tm, d), BF16)],
        compiler_params=_params("parallel", "arbitrary"), name="norm_proj")(*args)


def _log_sigmoid(x):
    return jnp.minimum(x, 0.0) - jnp.log1p(jnp.exp(-jnp.abs(x)))


def _mlstm_kernel(q_ref, k_ref, v_ref, o_ref, g_ref, bias_ref, hnorm_ref, out_ref, c_ref, n_ref, m_ref):
    L, H, DQK, DV = MLSTM_CHUNK, MLSTM_HEADS, MLSTM_DQK, MLSTM_DV
    scale = DQK ** -0.5

    @pl.when(pl.program_id(1) == 0)
    def _():
        c_ref[...] = jnp.zeros_like(c_ref)
        n_ref[...] = jnp.zeros_like(n_ref)
        m_ref[...] = jnp.zeros_like(m_ref)

    g = g_ref[...] + bias_ref[...]
    ls = _log_sigmoid(g)
    g_t = g.T
    ls_t = ls.T
    row = lax.broadcasted_iota(jnp.int32, (L, L), 0)
    col = lax.broadcasted_iota(jnp.int32, (L, L), 1)
    causal = col <= row

    for h in range(H):
        q = q_ref[:, h * DQK:(h + 1) * DQK]
        k = k_ref[:, h * DQK:(h + 1) * DQK]
        v = v_ref[:, h * DV:(h + 1) * DV]
        i_col = g[:, h:h + 1]
        i_row = g_t[h:h + 1, :]
        lf_col = ls[:, H + h:H + h + 1]
        lf_row = ls_t[H + h:H + h + 1, :]
        b_col = jnp.sum(jnp.where(causal, lf_row, 0.0), axis=1, keepdims=True)
        b_row = jnp.sum(jnp.where(row <= col, lf_col, 0.0), axis=0, keepdims=True)
        m_prev = m_ref[h][:, 0:1]
        c_prev = c_ref[h]
        n_prev = n_ref[h]

        d_log = jnp.where(causal, b_col - b_row + i_row, -jnp.inf)
        inter = b_col + m_prev
        m_t = jnp.maximum(inter, jnp.max(d_log, axis=1, keepdims=True))
        w_intra = jnp.exp(d_log - m_t)
        w_inter = jnp.exp(inter - m_t) * scale
        qk = lax.dot_general(q, k, (((1,), (1,)), ((), ())), preferred_element_type=F32)
        s = qk * scale * w_intra
        num = jnp.dot(s.astype(BF16), v, preferred_element_type=F32)
        num = num + w_inter * jnp.dot(q, c_prev.astype(BF16), preferred_element_type=F32)
        den = jnp.sum(s, axis=1, keepdims=True) + w_inter * jnp.sum(q.astype(F32) * n_prev, axis=1, keepdims=True)
        hout = num / jnp.maximum(jnp.abs(den), jnp.exp(-m_t))

        b_last = b_row[:, L - 1:L]
        st_row = b_last - b_row + i_row
        m_new = jnp.maximum(b_last + m_prev, jnp.max(st_row, axis=1, keepdims=True))
        w_st_col = jnp.exp(b_last - b_col + i_col - m_new)
        decay = jnp.exp(b_last + m_prev - m_new)
        kw = k.astype(F32) * w_st_col
        c_ref[h] = decay * c_prev + lax.dot_general(kw.astype(BF16), v, (((0,), (0,)), ((), ())),
                                                    preferred_element_type=F32)
        n_ref[h] = decay * n_prev + jnp.sum(kw, axis=0, keepdims=True)
        m_ref[h] = jnp.broadcast_to(m_new, (1, LANES))

        hn = hout * lax.rsqrt(jnp.mean(hout * hout, axis=-1, keepdims=True) + EPS) * hnorm_ref[h:h + 1, :]
        og = o_ref[:, h * DV:(h + 1) * DV].astype(F32)
        out_ref[:, h * DV:(h + 1) * DV] = (hn * jax.nn.sigmoid(og)).astype(out_ref.dtype)


def mlstm(proj, gates, gate_bias, head_norm, batch, seq):
    L, H, DQK, DV = MLSTM_CHUNK, MLSTM_HEADS, MLSTM_DQK, MLSTM_DV
    nc = seq // L
    qk_w, v_w = H * DQK, H * DV
    row = lambda b, c: b * nc + c
    return pl.pallas_call(
        _mlstm_kernel, grid=(batch, nc),
        in_specs=[pl.BlockSpec((L, qk_w), lambda b, c: (row(b, c), 0)),
                  pl.BlockSpec((L, qk_w), lambda b, c: (row(b, c), 1)),
                  pl.BlockSpec((L, v_w), lambda b, c: (row(b, c), 1)),
                  pl.BlockSpec((L, v_w), lambda b, c: (row(b, c), 2)),
                  pl.BlockSpec((L, LANES), lambda b, c: (row(b, c), 0)),
                  pl.BlockSpec((1, LANES), lambda b, c: (0, 0)),
                  pl.BlockSpec((H, DV), lambda b, c: (0, 0))],
        out_specs=pl.BlockSpec((L, v_w), lambda b, c: (row(b, c), 0)),
        out_shape=jax.ShapeDtypeStruct((batch * seq, v_w), BF16),
        scratch_shapes=[pltpu.VMEM((H, DQK, DV), F32), pltpu.VMEM((H, 1, DQK), F32), pltpu.VMEM((H, 1, LANES), F32)],
        compiler_params=_params("parallel", "arbitrary"), name="mlstm")(
            proj, proj, proj, proj, gates, gate_bias, head_norm)


def _mem_attn_kernel(q_ref, k_ref, v_ref, qn_ref, kn_ref, out_ref):
    hd = MEM_HEAD_DIM
    scale = hd ** -0.5
    for h in range(MEM_HEADS):
        sl = slice(h * hd, (h + 1) * hd)
        q = q_ref[:, sl].astype(F32)
        q = q * lax.rsqrt(jnp.mean(q * q, axis=-1, keepdims=True) + EPS) * qn_ref[...]
        k = k_ref[:, sl].astype(F32)
        k = k * lax.rsqrt(jnp.mean(k * k, axis=-1, keepdims=True) + EPS) * kn_ref[...]
        s = lax.dot_general(q.astype(BF16), k.astype(BF16), (((1,), (1,)), ((), ())),
                            preferred_element_type=F32) * scale
        p = jnp.exp(s - jnp.max(s, axis=-1, keepdims=True))
        p = p / jnp.sum(p, axis=-1, keepdims=True)
        out_ref[:, sl] = jnp.dot(p.astype(BF16), v_ref[:, sl], preferred_element_type=F32).astype(out_ref.dtype)


def mem_attn(proj, q_col_block, mkv, layer, qnorm, knorm, batch, seq, slots, tm):
    width = MEM_HEADS * MEM_HEAD_DIM
    nt = seq // tm
    return pl.pallas_call(
        _mem_attn_kernel, grid=(batch, nt),
        in_specs=[pl.BlockSpec((tm, width), lambda b, i: (b * nt + i, q_col_block)),
                  pl.BlockSpec((slots, width), lambda b, i: (b, 2 * layer)),
                  pl.BlockSpec((slots, width), lambda b, i: (b, 2 * layer + 1)),
                  pl.BlockSpec((1, MEM_HEAD_DIM), lambda b, i: (0, 0)),
                  pl.BlockSpec((1, MEM_HEAD_DIM), lambda b, i: (0, 0))],
        out_specs=pl.BlockSpec((tm, width), lambda b, i: (b * nt + i, 0)),
        out_shape=jax.ShapeDtypeStruct((batch * seq, width), BF16),
        compiler_params=_params("parallel", "parallel"), name="mem_attn")(proj, mkv, mkv, qnorm, knorm)


def _out_proj_kernel(a_ref, m_ref, wa_ref, wm_ref, x_ref, fn_ref, wr_ref, br_ref, xo_ref, route_ref):
    y = jnp.dot(a_ref[...], wa_ref[...], preferred_element_type=F32)
    y = y + jnp.dot(m_ref[...], wm_ref[...], preferred_element_type=F32)
    x = x_ref[...] + y
    xo_ref[...] = x
    hn = (x * lax.rsqrt(jnp.mean(x * x, axis=-1, keepdims=True) + EPS) * fn_ref[...]).astype(BF16)
    lg = jnp.dot(hn, wr_ref[...], preferred_element_type=F32) + br_ref[...]
    lane = lax.broadcasted_iota(jnp.int32, lg.shape, 1).astype(F32)
    neg = -jnp.inf
    gl = jnp.where(lane < N_GROUPS, lg, neg)
    gmax = jnp.max(gl, axis=1, keepdims=True)
    g_w = 1.0 / jnp.sum(jnp.exp(gl - gmax), axis=1, keepdims=True)
    g_idx = jnp.min(jnp.where(gl == gmax, lane, float(LANES)), axis=1, keepdims=True)
    e_lo = N_GROUPS + EXPERTS_PER_GROUP * g_idx
    el = jnp.where((lane >= e_lo) & (lane < e_lo + EXPERTS_PER_GROUP), lg, neg)
    max1 = jnp.max(el, axis=1, keepdims=True)
    e1 = jnp.min(jnp.where(el == max1, lane, float(LANES)), axis=1, keepdims=True)
    el2 = jnp.where(lane == e1, neg, el)
    max2 = jnp.max(el2, axis=1, keepdims=True)
    e2 = jnp.min(jnp.where(el2 == max2, lane, float(LANES)), axis=1, keepdims=True)
    p2 = jnp.exp(max2 - max1)
    w1 = g_w / (1.0 + p2)
    w2 = g_w * p2 / (1.0 + p2)
    route = jnp.where(lane == 0, w1, jnp.where(lane == 1, w2, jnp.where(
        lane == 2, e1 - N_GROUPS, jnp.where(lane == 3, e2 - N_GROUPS, 0.0))))
    route_ref[...] = route


def out_proj(a, m, w_a, w_m, x, ffn_norm, w_router, b_router, tm):
    t, d = x.shape
    ka, km = a.shape[1], m.shape[1]
    return pl.pallas_call(
        _out_proj_kernel, grid=(t // tm,),
        in_specs=[pl.BlockSpec((tm, ka), lambda i: (i, 0)), pl.BlockSpec((tm, km), lambda i: (i, 0)),
                  pl.BlockSpec((ka, d), lambda i: (0, 0)), pl.BlockSpec((km, d), lambda i: (0, 0)),
                  pl.BlockSpec((tm, d), lambda i: (i, 0)), pl.BlockSpec((1, d), lambda i: (0, 0)),
                  pl.BlockSpec((d, LANES), lambda i: (0, 0)), pl.BlockSpec((1, LANES), lambda i: (0, 0))],
        out_specs=[pl.BlockSpec((tm, d), lambda i: (i, 0)), pl.BlockSpec((tm, LANES), lambda i: (i, 0))],
        out_shape=[jax.ShapeDtypeStruct((t, d), F32), jax.ShapeDtypeStruct((t, LANES), F32)],
        compiler_params=_params("parallel"), name="out_proj")(a, m, w_a, w_m, x, ffn_norm, w_router, b_router)


def _moe_kernel(blk_exp_ref, tok_ref, dst_ref, nvalid_ref, nused_ref,
                x_hbm, fn_ref, wt_ref, wg_ref, wu_ref, wd_ref, out_hbm, xbuf, ybuf, gsem, ssem):
    del blk_exp_ref
    bm = MOE_BLOCK
    i = pl.program_id(0)
    n_used = nused_ref[0]

    def row_gather(blk, slot, r):
        return pltpu.make_async_copy(x_hbm.at[pl.ds(tok_ref[blk * bm + r], 1)], xbuf.at[slot, pl.ds(r, 1)],
                                     gsem.at[slot])

    def row_scatter(blk, r):
        return pltpu.make_async_copy(ybuf.at[pl.ds(r, 1)], out_hbm.at[pl.ds(dst_ref[blk * bm + r], 1)], ssem.at[0])

    def start_rows(copy, n):
        groups = n // MOE_DMA_UNROLL

        def group(g, c):
            for u in range(MOE_DMA_UNROLL):
                copy(g * MOE_DMA_UNROLL + u).start()
            return c

        lax.fori_loop(0, groups, group, 0)
        if not isinstance(n, int) or n % MOE_DMA_UNROLL:
            lax.fori_loop(groups * MOE_DMA_UNROLL, n, lambda r, c: (copy(r).start(), c)[1], 0)

    def start_gather(blk, slot):
        start_rows(functools.partial(row_gather, blk, slot), bm)

    def wait_gather(blk, slot):
        del blk
        pltpu.make_async_copy(x_hbm.at[pl.ds(0, bm)], xbuf.at[slot], gsem.at[slot]).wait()

    def start_scatter(blk):
        start_rows(functools.partial(row_scatter, blk), nvalid_ref[blk])

    def wait_scatter(blk):
        n = nvalid_ref[blk]
        n8 = pl.multiple_of(n // 8 * 8, 8)

        @pl.when(n8 > 0)
        def _():
            pltpu.make_async_copy(ybuf.at[pl.ds(0, n8)], out_hbm.at[pl.ds(0, n8)], ssem.at[0]).wait()

        lax.fori_loop(n8, n, lambda r, c: (row_scatter(blk, r).wait(), c)[1], 0)

    @pl.when(i == 0)
    def _():
        start_gather(0, 0)

    @pl.when(i < n_used)
    def _():
        slot = i % 2

        @pl.when(i + 1 < n_used)
        def _():
            start_gather(i + 1, 1 - slot)

        wait_gather(i, slot)
        x = xbuf[slot]
        xn = (x * lax.rsqrt(jnp.mean(x * x, axis=-1, keepdims=True) + EPS) * fn_ref[...]).astype(BF16)
        gate = jnp.dot(xn, wg_ref[...], preferred_element_type=F32)
        up = jnp.dot(xn, wu_ref[...], preferred_element_type=F32)
        act = (gate * jax.nn.sigmoid(gate) * up).astype(BF16)
        y = jnp.dot(act, wd_ref[...], preferred_element_type=F32) * wt_ref[...]

        @pl.when(i > 0)
        def _():
            wait_scatter(i - 1)

        ybuf[...] = y
        start_scatter(i)

        @pl.when(i == n_used - 1)
        def _():
            wait_scatter(i)


def moe_experts(x, ffn_norm, tables, w_gate, w_up, w_down):
    bm = MOE_BLOCK
    t, d = x.shape
    f = w_gate.shape[2]
    tok_buf, dst_buf, wt_buf, blk_exp, n_valid, n_used = tables
    n_blocks = blk_exp.shape[0]
    const = lambda i, *_: (0, 0)
    expert = lambda i, be, *_: (be[i], 0, 0)
    grid_spec = pltpu.PrefetchScalarGridSpec(
        num_scalar_prefetch=5, grid=(n_blocks,),
        in_specs=[pl.BlockSpec(memory_space=pl.ANY),
                  pl.BlockSpec((1, d), const),
                  pl.BlockSpec((bm, 1), lambda i, *_: (i, 0)),
                  pl.BlockSpec((None, d, f), expert),
                  pl.BlockSpec((None, d, f), expert),
                  pl.BlockSpec((None, f, d), expert)],
        out_specs=pl.BlockSpec(memory_space=pl.ANY),
        scratch_shapes=[pltpu.VMEM((2, bm, d), F32), pltpu.VMEM((bm, d), F32),
                        pltpu.SemaphoreType.DMA((2,)), pltpu.SemaphoreType.DMA((1,))])
    return pl.pallas_call(
        _moe_kernel, grid_spec=grid_spec, out_shape=jax.ShapeDtypeStruct((TOP_K * t, d), F32),
        compiler_params=_params("arbitrary"), name="moe_experts")(
            blk_exp, tok_buf, dst_buf, n_valid, n_used, x, ffn_norm, wt_buf, w_gate, w_up, w_down)


def moe_tables(route, n_blocks):
    bm = MOE_BLOCK
    t = route.shape[0]
    a = TOP_K * t
    eid = route[:, 2:4].astype(jnp.int32).T.reshape(a)
    wts = route[:, 0:2].T.reshape(a)
    order = jnp.argsort(eid, stable=True).astype(jnp.int32)
    counts = jnp.sum((eid[:, None] == jnp.arange(N_EXPERTS, dtype=jnp.int32)[None, :]).astype(jnp.int32), axis=0)
    padded = (counts + bm - 1) // bm * bm
    start = jnp.cumsum(counts) - counts
    pend = jnp.cumsum(padded)
    pstart = pend - padded
    p = n_blocks * bm
    blk_start = jnp.arange(n_blocks, dtype=jnp.int32) * bm
    blk_exp = jnp.minimum(jnp.searchsorted(pend, blk_start, side='right'), N_EXPERTS - 1).astype(jnp.int32)
    n_valid = jnp.clip(counts[blk_exp] - (blk_start - pstart[blk_exp]), 0, bm).astype(jnp.int32)
    n_used = (pend[-1] // bm).astype(jnp.int32).reshape(1)
    slot = jnp.arange(p, dtype=jnp.int32)
    slot_exp = jnp.repeat(blk_exp, bm)
    rank = slot - pstart[slot_exp]
    valid = rank < counts[slot_exp]
    src = order[jnp.clip(start[slot_exp] + rank, 0, a - 1)]
    tok_buf = jnp.where(valid, src % t, 0)
    wt_buf = jnp.where(valid, wts[src], 0.0)
    dst_buf = jnp.where(valid, src, 0)
    return tok_buf, dst_buf, wt_buf.reshape(p, 1), blk_exp, n_valid, n_used


def moe_layer(x_mid, route, ffn_norm, w_gate, w_up, w_down):
    t = x_mid.shape[0]
    a = TOP_K * t
    n_blocks = -(-a // MOE_BLOCK) + N_EXPERTS
    return moe_experts(x_mid, ffn_norm, moe_tables(route, n_blocks),
                       w_gate.astype(BF16), w_up.astype(BF16), w_down.astype(BF16))


DIFF_TQ = 512
DIFF_KS = 256
DIFF_CW = 256
DIFF_AHEAD = 6
DIFF_ONES = 16


def _diff_attn_kernel(q_ref, k_ref, v_ref, lam_ref, subln_ref, out_ref, vt_ref, qst_ref, m_ref, acc_ref,
                      *, tq, ks, cw, ahead, lambda_init):
    hd = DIFF_HD
    dv = DIFF_DV
    seq = k_ref.shape[0]
    qi = pl.program_id(2)
    n_cols = 2 * tq

    @pl.when(qi == 0)
    def _():
        for j in range(seq // tq):
            vt_ref[0:dv, j * tq:(j + 1) * tq] = v_ref[j * tq:(j + 1) * tq, :].astype(F32).T.astype(BF16)
        vt_ref[dv:dv + DIFF_ONES, :] = jnp.ones((DIFF_ONES, seq), BF16)

    qt = q_ref[...].astype(F32).T
    row = lax.broadcasted_iota(jnp.int32, qt.shape, 0)
    qst_ref[:, 0:tq] = jnp.where(row < hd, qt, 0.0).astype(BF16)
    qst_ref[:, tq:n_cols] = jnp.where(row >= hd, qt, 0.0).astype(BF16)
    m_ref[...] = jnp.full_like(m_ref, -jnp.inf)
    acc_ref[...] = jnp.zeros_like(acc_ref)

    def tile(ki, diagonal):
        off = pl.multiple_of(ki * tq, tq)
        items = []
        for kc in range(tq // ks):
            for c in range(n_cols // cw):
                q_lo = (c * cw) % tq
                if diagonal and q_lo + cw <= kc * ks:
                    continue
                items.append((kc, c, diagonal and (kc + 1) * ks - 1 > q_lo))

        def scores(kc, c):
            k = k_ref[pl.ds(off + kc * ks, ks), :]
            return jnp.dot(k, qst_ref[:, c * cw:(c + 1) * cw], preferred_element_type=F32)

        pending = [scores(kc, c) for kc, c, _ in items[:ahead]]
        for n, (kc, c, masked) in enumerate(items):
            s = pending.pop(0)
            if n + ahead < len(items):
                pending.append(scores(*items[n + ahead][:2]))
            cols = slice(c * cw, (c + 1) * cw)
            if masked:
                kpos = kc * ks + lax.broadcasted_iota(jnp.int32, s.shape, 0)
                qpos = (c * cw) % tq + lax.broadcasted_iota(jnp.int32, s.shape, 1)
                s = jnp.where(kpos <= qpos, s, -jnp.inf)
            m_prev = m_ref[:, cols]
            m_new = jnp.maximum(m_prev, jnp.max(s, axis=0, keepdims=True))
            alpha = jnp.exp2(m_prev - m_new)
            p = jnp.exp2(s - m_new)
            vt = vt_ref[:, pl.ds(off + kc * ks, ks)]
            acc_ref[:, cols] = alpha * acc_ref[:, cols] + jnp.dot(vt, p.astype(BF16), preferred_element_type=F32)
            m_ref[:, cols] = m_new

    lax.fori_loop(0, qi, lambda ki, c: (tile(ki, False), c)[1], 0)
    tile(qi, True)

    lv = lam_ref[...]
    lam = (jnp.exp(jnp.sum(lv[0:1] * lv[1:2], axis=1, keepdims=True))
           - jnp.exp(jnp.sum(lv[2:3] * lv[3:4], axis=1, keepdims=True)) + lambda_init)
    o = (acc_ref[0:dv, 0:tq] / acc_ref[dv:dv + 1, 0:tq]
         - lam * (acc_ref[0:dv, tq:n_cols] / acc_ref[dv:dv + 1, tq:n_cols]))
    o = o * lax.rsqrt(jnp.mean(o * o, axis=0, keepdims=True) + EPS) * subln_ref[...] * (1.0 - lambda_init)
    out_ref[...] = o.T.astype(out_ref.dtype)


def diff_attn(proj, lam_vecs, subln_col, lambda_init, batch, seq, tq):
    nh = DIFF_HEADS
    nq = seq // tq
    kern = functools.partial(_diff_attn_kernel, tq=tq, ks=min(DIFF_KS, tq), cw=min(DIFF_CW, tq), ahead=DIFF_AHEAD,
                             lambda_init=lambda_init)
    return pl.pallas_call(
        kern, grid=(batch, nh, nq),
        in_specs=[pl.BlockSpec((tq, LANES), lambda b, h, i: (b * nq + i, 2 * nh + h)),
                  pl.BlockSpec((seq, LANES), lambda b, h, i: (b, h)),
                  pl.BlockSpec((seq, LANES), lambda b, h, i: (b, nh + h)),
                  pl.BlockSpec((4, DIFF_HD), lambda b, h, i: (0, 0)),
                  pl.BlockSpec((DIFF_DV, 1), lambda b, h, i: (0, 0))],
        out_specs=pl.BlockSpec((tq, DIFF_DV), lambda b, h, i: (b * nq + i, h)),
        out_shape=jax.ShapeDtypeStruct((batch * seq, nh * DIFF_DV), BF16),
        scratch_shapes=[pltpu.VMEM((DIFF_DV + DIFF_ONES, seq), BF16), pltpu.VMEM((LANES, 2 * tq), BF16),
                        pltpu.VMEM((1, 2 * tq), F32), pltpu.VMEM((DIFF_DV + DIFF_ONES, 2 * tq), F32)],
        compiler_params=_params("parallel", "parallel", "arbitrary"), name="diff_attn")(
            proj, proj, proj, lam_vecs, subln_col)


def _combine_kernel(x_ref, a_ref, b_ref, out_ref):
    out_ref[...] = x_ref[...] + a_ref[...] + b_ref[...]


def combine(x_mid, moe_out, tm):
    t, d = x_mid.shape
    nb = t // tm
    return pl.pallas_call(
        _combine_kernel, grid=(nb,),
        in_specs=[pl.BlockSpec((tm, d), lambda i: (i, 0)), pl.BlockSpec((tm, d), lambda i: (i, 0)),
                  pl.BlockSpec((tm, d), lambda i: (i + nb, 0))],
        out_specs=pl.BlockSpec((tm, d), lambda i: (i, 0)),
        out_shape=jax.ShapeDtypeStruct((t, d), F32),
        compiler_params=_params("parallel"), name="combine")(x_mid, moe_out, moe_out)


def _row_tile(n, want):
    while n % want:
        want //= 2
    return want


def _router_weights(w_group, b_group, w_expert, b_expert):
    d = w_group.shape[0]
    pad = LANES - N_GROUPS - N_EXPERTS
    w = jnp.concatenate([w_group, w_expert, jnp.zeros((d, pad), F32)], axis=1).astype(BF16)
    b = jnp.concatenate([b_group, b_expert, jnp.zeros((pad,), F32)]).reshape(1, LANES)
    return w, b


def kernel(x, mem, positions, attn_norm, mem_norm, mem_w_kv, mem_qnorm, mem_knorm, a_w_in, a_gate_bias,
           a_head_norm, a_w_out, kv_norm, kv_w, kv_knorm, b_w_in, b_qnorm, b_lambda, b_subln, b_w_out, ffn_norm,
           moe_w_group, moe_b_group, moe_w_expert, moe_b_expert, moe_w_gate, moe_w_up, moe_w_down):
    batch, seq, d = x.shape
    slots = mem.shape[1]
    t = batch * seq
    tm = _row_tile(t, 512)
    a_qk = MLSTM_HEADS * MLSTM_DQK
    a_v = MLSTM_HEADS * MLSTM_DV
    mem_w = MEM_HEADS * MEM_HEAD_DIM
    diff_qk = 2 * DIFF_HEADS * DIFF_HD
    xt = x.reshape(t, d)

    memt = mem.reshape(batch * slots, d)
    w_mkv = jnp.concatenate([mem_w_kv[0], mem_w_kv[1]], axis=1).astype(BF16)
    (mkv,) = norm_proj([memt], [0], batch * slots, mem_norm, w_mkv, tm=_row_tile(batch * slots, 512), tn=512,
                       norm_bounds=(2 * mem_w // 512,))

    g_lo = 2 * a_qk + 2 * a_v
    w_in = a_w_in[0]
    w_main = jnp.concatenate([w_in[:, :g_lo], w_in[:, g_lo + 2 * MLSTM_HEADS:]], axis=1).astype(BF16)
    w_gates = jnp.pad(w_in[:, g_lo:g_lo + 2 * MLSTM_HEADS], ((0, 0), (0, LANES - 2 * MLSTM_HEADS))).astype(BF16)
    proj0, gates = norm_proj([xt], [0], t, attn_norm[0:1], w_main, tm=tm, tn=512, w_aux=w_gates)
    gate_bias = jnp.pad(a_gate_bias[0], (0, LANES - 2 * MLSTM_HEADS)).reshape(1, LANES)
    hm = mlstm(proj0, gates, gate_bias, a_head_norm[0], batch, seq)
    mo0 = mem_attn(proj0, g_lo // mem_w, mkv, 0, mem_qnorm[0:1], mem_knorm[0:1], batch, seq, slots, _row_tile(seq, 512))
    w_r0, b_r0 = _router_weights(moe_w_group[0], moe_b_group[0], moe_w_expert[0], moe_b_expert[0])
    w_out0 = a_w_out[0].astype(BF16)
    x_mid0, route0 = out_proj(hm, mo0, w_out0[:a_v], w_out0[a_v:], xt, ffn_norm[0:1], w_r0, b_r0, tm)
    moe0 = moe_layer(x_mid0, route0, ffn_norm[0:1], moe_w_gate[0], moe_w_up[0], moe_w_down[0])

    half = DIFF_HD // 2
    inv_freq = ROPE_THETA ** (-jnp.arange(half, dtype=F32) / half)
    ang = positions.astype(F32).reshape(t, 1) * inv_freq[None, :]
    lane = jnp.arange(LANES)
    cos = jnp.cos(ang)[:, lane % half]
    sin = jnp.sin(ang)[:, lane % half] * jnp.where((lane % DIFF_HD) < half, -1.0, 1.0)[None, :]
    tn = 512
    w1 = jnp.concatenate([kv_w, b_w_in[0]], axis=1).astype(BF16)
    n_tiles = w1.shape[1] // tn
    k_tiles = diff_qk // tn
    q_lo = kv_w.shape[1] // tn
    reps = LANES // DIFF_HD
    ew = jnp.zeros((n_tiles, 1, LANES), F32)
    ew = ew.at[0:k_tiles].set(jnp.tile(kv_knorm, reps)[None, None, :])
    q_scale = DIFF_HD ** -0.5 * math.log2(math.e)
    ew = ew.at[q_lo:q_lo + k_tiles].set(jnp.tile(b_qnorm[0], reps)[None, None, :] * q_scale)
    norms1 = jnp.stack([kv_norm, attn_norm[1]])
    nb = t // tm
    proj1, x1 = norm_proj([x_mid0, moe0, moe0], [0, 0, nb], t, norms1, w1, tm=tm, tn=tn, norm_bounds=(q_lo,),
                          rope=(cos, sin, ew), rope_ranges=((0, k_tiles), (q_lo, q_lo + k_tiles)), write_sum=True)

    lambda_init = 0.8 - 0.6 * math.exp(-0.3 * 1)
    oa = diff_attn(proj1, b_lambda[0], b_subln[0].reshape(DIFF_DV, 1), lambda_init, batch, seq,
                   _row_tile(seq, DIFF_TQ))
    mo1 = mem_attn(proj1, (kv_w.shape[1] + diff_qk) // mem_w, mkv, 1, mem_qnorm[1:2], mem_knorm[1:2], batch, seq,
                   slots, _row_tile(seq, 512))
    w_r1, b_r1 = _router_weights(moe_w_group[1], moe_b_group[1], moe_w_expert[1], moe_b_expert[1])
    w_out1 = b_w_out[0].astype(BF16)
    n_oa = DIFF_HEADS * DIFF_DV
    x_mid1, route1 = out_proj(oa, mo1, w_out1[:n_oa], w_out1[n_oa:], x1, ffn_norm[1:2], w_r1, b_r1, tm)
    moe1 = moe_layer(x_mid1, route1, ffn_norm[1:2], moe_w_gate[1], moe_w_up[1], moe_w_down[1])
    out = combine(x_mid1, moe1, tm)
    return out.reshape(batch, seq, d)
```

```python
import functools
import math

import jax
import jax.numpy as jnp
from jax import lax
from jax.experimental import pallas as pl
from jax.experimental.pallas import tpu as pltpu

F32 = jnp.float32
BF16 = jnp.bfloat16
EPS = 1e-6
LANES = 128
VMEM_LIMIT_BYTES = 56 * 1024 * 1024

MLSTM_HEADS = 6
MLSTM_DQK = 128
MLSTM_DV = 256
MLSTM_CHUNK = 128
MEM_HEADS = 4
MEM_HEAD_DIM = 128
DIFF_HEADS = 12
DIFF_HD = 64
DIFF_DV = 128
ROPE_THETA = 10000.0
N_GROUPS = 4
EXPERTS_PER_GROUP = 8
N_EXPERTS = N_GROUPS * EXPERTS_PER_GROUP
TOP_K = 2
MOE_BLOCK = 256
MOE_DMA_UNROLL = 8


def _params(*semantics):
    return pltpu.CompilerParams(dimension_semantics=semantics, vmem_limit_bytes=VMEM_LIMIT_BYTES)


def _norm_proj_kernel(*refs, n_parts, write_sum, norm_bounds, rope_ranges, has_aux, hd):
    it = iter(refs)
    parts = [next(it) for _ in range(n_parts)]
    normw_ref = next(it)
    w_ref = next(it)
    if rope_ranges:
        cos_ref, sin_ref, ew_ref = next(it), next(it), next(it)
    if has_aux:
        waux_ref = next(it)
    out_ref = next(it)
    if write_sum:
        xsum_ref = next(it)
    if has_aux:
        aux_ref = next(it)
    h_ref = next(it)

    j = pl.program_id(1)
    n_norm = len(norm_bounds) + 1

    @pl.when(j == 0)
    def _():
        x = parts[0][...]
        for p in parts[1:]:
            x = x + p[...]
        if write_sum:
            xsum_ref[...] = x
        xn = x * lax.rsqrt(jnp.mean(x * x, axis=-1, keepdims=True) + EPS)
        for n in range(n_norm):
            h_ref[n] = (xn * normw_ref[n:n + 1, :]).astype(BF16)
        if has_aux:
            aux_ref[...] = jnp.dot(h_ref[0], waux_ref[...], preferred_element_type=F32)

    sel = jnp.int32(0)
    for b in norm_bounds:
        sel = sel + (j >= b).astype(jnp.int32)
    acc = jnp.dot(h_ref[sel], w_ref[...], preferred_element_type=F32)

    if not rope_ranges:
        out_ref[...] = acc.astype(out_ref.dtype)
        return

    is_rope = jnp.bool_(False)
    for lo, hi in rope_ranges:
        is_rope = is_rope | ((j >= lo) & (j < hi))

    @pl.when(is_rope)
    def _():
        ew = ew_ref[j]
        cs = cos_ref[...]
        sn = sin_ref[...]
        r = lax.broadcasted_iota(jnp.int32, (2 * LANES, LANES), 0)
        c = lax.broadcasted_iota(jnp.int32, (2 * LANES, LANES), 1)
        seg = ((r % LANES) // hd == c // hd).astype(BF16)
        lane = lax.broadcasted_iota(jnp.int32, (acc.shape[0], LANES), 1)
        first_half = (lane % hd) < (hd // 2)
        for s in range(acc.shape[1] // LANES):
            t = acc[:, s * LANES:(s + 1) * LANES]
            tt = t * t
            hi_part = tt.astype(BF16)
            lo_part = (tt - hi_part.astype(F32)).astype(BF16)
            ss = jnp.dot(jnp.concatenate([hi_part, lo_part], axis=1), seg, preferred_element_type=F32)
            y = t * lax.rsqrt(ss * (1.0 / hd) + EPS) * ew
            rot = jnp.where(first_half, pltpu.roll(y, LANES - hd // 2, 1), pltpu.roll(y, hd // 2, 1))
            out_ref[:, s * LANES:(s + 1) * LANES] = (y * cs + rot * sn).astype(out_ref.dtype)

    @pl.when(jnp.logical_not(is_rope))
    def _():
        out_ref[...] = acc.astype(out_ref.dtype)


def norm_proj(x_parts, part_blocks, n_rows, norm_w, w, *, tm, tn, norm_bounds=(), rope=None,
              rope_ranges=(), w_aux=None, write_sum=False):
    d = w.shape[0]
    n = w.shape[1]
    n_norm = norm_w.shape[0]
    grid = (n_rows // tm, n // tn)
    in_specs = [pl.BlockSpec((tm, d), functools.partial(lambda i, j, o: (i + o, 0), o=o)) for o in part_blocks]
    args = list(x_parts)
    in_specs += [pl.BlockSpec((n_norm, d), lambda i, j: (0, 0)), pl.BlockSpec((d, tn), lambda i, j: (0, j))]
    args += [norm_w, w]
    if rope_ranges:
        cos, sin, ew = rope
        in_specs += [pl.BlockSpec((tm, LANES), lambda i, j: (i, 0)), pl.BlockSpec((tm, LANES), lambda i, j: (i, 0)),
                     pl.BlockSpec(ew.shape, lambda i, j: (0, 0, 0))]
        args += [cos, sin, ew]
    if w_aux is not None:
        in_specs.append(pl.BlockSpec((d, LANES), lambda i, j: (0, 0)))
        args.append(w_aux)
    out_shape = [jax.ShapeDtypeStruct((n_rows, n), BF16)]
    out_specs = [pl.BlockSpec((tm, tn), lambda i, j: (i, j))]
    if write_sum:
        out_shape.append(jax.ShapeDtypeStruct((n_rows, d), F32))
        out_specs.append(pl.BlockSpec((tm, d), lambda i, j: (i, 0)))
    if w_aux is not None:
        out_shape.append(jax.ShapeDtypeStruct((n_rows, LANES), F32))
        out_specs.append(pl.BlockSpec((tm, LANES), lambda i, j: (i, 0)))
    kern = functools.partial(_norm_proj_kernel, n_parts=len(x_parts), write_sum=write_sum,
                             norm_bounds=tuple(norm_bounds), rope_ranges=tuple(rope_ranges),
                             has_aux=w_aux is not None, hd=DIFF_HD)
    return pl.pallas_call(
        kern, grid=grid, in_specs=in_specs, out_specs=out_specs, out_shape=out_shape,
        scratch_shapes=[pltpu.VMEM((n_norm, tm, d), BF16)],
        compiler_params=_params("parallel", "arbitrary"), name="norm_proj")(*args)


def _log_sigmoid(x):
    return jnp.minimum(x, 0.0) - jnp.log1p(jnp.exp(-jnp.abs(x)))


def _mlstm_kernel(q_ref, k_ref, v_ref, o_ref, g_ref, bias_ref, hnorm_ref, out_ref, c_ref, n_ref, m_ref):
    L, H, DQK, DV = MLSTM_CHUNK, MLSTM_HEADS, MLSTM_DQK, MLSTM_DV
    scale = DQK ** -0.5

    @pl.when(pl.program_id(1) == 0)
    def _():
        c_ref[...] = jnp.zeros_like(c_ref)
        n_ref[...] = jnp.zeros_like(n_ref)
        m_ref[...] = jnp.zeros_like(m_ref)

    g = g_ref[...] + bias_ref[...]
    ls = _log_sigmoid(g)
    g_t = g.T
    ls_t = ls.T
    row = lax.broadcasted_iota(jnp.int32, (L, L), 0)
    col = lax.broadcasted_iota(jnp.int32, (L, L), 1)
    causal = col <= row

    for h in range(H):
        q = q_ref[:, h * DQK:(h + 1) * DQK]
        k = k_ref[:, h * DQK:(h + 1) * DQK]
        v = v_ref[:, h * DV:(h + 1) * DV]
        i_col = g[:, h:h + 1]
        i_row = g_t[h:h + 1, :]
        lf_col = ls[:, H + h:H + h + 1]
        lf_row = ls_t[H + h:H + h + 1, :]
        b_col = jnp.sum(jnp.where(causal, lf_row, 0.0), axis=1, keepdims=True)
        b_row = jnp.sum(jnp.where(row <= col, lf_col, 0.0), axis=0, keepdims=True)
        m_prev = m_ref[h][:, 0:1]
        c_prev = c_ref[h]
        n_prev = n_ref[h]

        d_log = jnp.where(causal, b_col - b_row + i_row, -jnp.inf)
        inter = b_col + m_prev
        m_t = jnp.maximum(inter, jnp.max(d_log, axis=1, keepdims=True))
        w_intra = jnp.exp(d_log - m_t)
        w_inter = jnp.exp(inter - m_t) * scale
        qk = lax.dot_general(q, k, (((1,), (1,)), ((), ())), preferred_element_type=F32)
        s = qk * scale * w_intra
        num = jnp.dot(s.astype(BF16), v, preferred_element_type=F32)
        num = num + w_inter * jnp.dot(q, c_prev.astype(BF16), preferred_element_type=F32)
        den = jnp.sum(s, axis=1, keepdims=True) + w_inter * jnp.sum(q.astype(F32) * n_prev, axis=1, keepdims=True)
        hout = num / jnp.maximum(jnp.abs(den), jnp.exp(-m_t))

        b_last = b_row[:, L - 1:L]
        st_row = b_last - b_row + i_row
        m_new = jnp.maximum(b_last + m_prev, jnp.max(st_row, axis=1, keepdims=True))
        w_st_col = jnp.exp(b_last - b_col + i_col - m_new)
        decay = jnp.exp(b_last + m_prev - m_new)
        kw = k.astype(F32) * w_st_col
        c_ref[h] = decay * c_prev + lax.dot_general(kw.astype(BF16), v, (((0,), (0,)), ((), ())),
                                                    preferred_element_type=F32)
        n_ref[h] = decay * n_prev + jnp.sum(kw, axis=0, keepdims=True)
        m_ref[h] = jnp.broadcast_to(m_new, (1, LANES))

        hn = hout * lax.rsqrt(jnp.mean(hout * hout, axis=-1, keepdims=True) + EPS) * hnorm_ref[h:h + 1, :]
        og = o_ref[:, h * DV:(h + 1) * DV].astype(F32)
        out_ref[:, h * DV:(h + 1) * DV] = (hn * jax.nn.sigmoid(og)).astype(out_ref.dtype)


def mlstm(proj, gates, gate_bias, head_norm, batch, seq):
    L, H, DQK, DV = MLSTM_CHUNK, MLSTM_HEADS, MLSTM_DQK, MLSTM_DV
    nc = seq // L
    qk_w, v_w = H * DQK, H * DV
    row = lambda b, c: b * nc + c
    return pl.pallas_call(
        _mlstm_kernel, grid=(batch, nc),
        in_specs=[pl.BlockSpec((L, qk_w), lambda b, c: (row(b, c), 0)),
                  pl.BlockSpec((L, qk_w), lambda b, c: (row(b, c), 1)),
                  pl.BlockSpec((L, v_w), lambda b, c: (row(b, c), 1)),
                  pl.BlockSpec((L, v_w), lambda b, c: (row(b, c), 2)),
                  pl.BlockSpec((L, LANES), lambda b, c: (row(b, c), 0)),
                  pl.BlockSpec((1, LANES), lambda b, c: (0, 0)),
                  pl.BlockSpec((H, DV), lambda b, c: (0, 0))],
        out_specs=pl.BlockSpec((L, v_w), lambda b, c: (row(b, c), 0)),
        out_shape=jax.ShapeDtypeStruct((batch * seq, v_w), BF16),
        scratch_shapes=[pltpu.VMEM((H, DQK, DV), F32), pltpu.VMEM((H, 1, DQK), F32), pltpu.VMEM((H, 1, LANES), F32)],
        compiler_params=_params("parallel", "arbitrary"), name="mlstm")(
            proj, proj, proj, proj, gates, gate_bias, head_norm)


def _mem_attn_kernel(q_ref, k_ref, v_ref, qn_ref, kn_ref, out_ref):
    hd = MEM_HEAD_DIM
    scale = hd ** -0.5
    for h in range(MEM_HEADS):
        sl = slice(h * hd, (h + 1) * hd)
        q = q_ref[:, sl].astype(F32)
        q = q * lax.rsqrt(jnp.mean(q * q, axis=-1, keepdims=True) + EPS) * qn_ref[...]
        k = k_ref[:, sl].astype(F32)
        k = k * lax.rsqrt(jnp.mean(k * k, axis=-1, keepdims=True) + EPS) * kn_ref[...]
        s = lax.dot_general(q.astype(BF16), k.astype(BF16), (((1,), (1,)), ((), ())),
                            preferred_element_type=F32) * scale
        p = jnp.exp(s - jnp.max(s, axis=-1, keepdims=True))
        p = p / jnp.sum(p, axis=-1, keepdims=True)
        out_ref[:, sl] = jnp.dot(p.astype(BF16), v_ref[:, sl], preferred_element_type=F32).astype(out_ref.dtype)


def mem_attn(proj, q_col_block, mkv, layer, qnorm, knorm, batch, seq, slots, tm):
    width = MEM_HEADS * MEM_HEAD_DIM
    nt = seq // tm
    return pl.pallas_call(
        _mem_attn_kernel, grid=(batch, nt),
        in_specs=[pl.BlockSpec((tm, width), lambda b, i: (b * nt + i, q_col_block)),
                  pl.BlockSpec((slots, width), lambda b, i: (b, 2 * layer)),
                  pl.BlockSpec((slots, width), lambda b, i: (b, 2 * layer + 1)),
                  pl.BlockSpec((1, MEM_HEAD_DIM), lambda b, i: (0, 0)),
                  pl.BlockSpec((1, MEM_HEAD_DIM), lambda b, i: (0, 0))],
        out_specs=pl.BlockSpec((tm, width), lambda b, i: (b * nt + i, 0)),
        out_shape=jax.ShapeDtypeStruct((batch * seq, width), BF16),
        compiler_params=_params("parallel", "parallel"), name="mem_attn")(proj, mkv, mkv, qnorm, knorm)


def _out_proj_kernel(a_ref, m_ref, wa_ref, wm_ref, x_ref, fn_ref, wr_ref, br_ref, xo_ref, route_ref):
    y = jnp.dot(a_ref[...], wa_ref[...], preferred_element_type=F32)
    y = y + jnp.dot(m_ref[...], wm_ref[...], preferred_element_type=F32)
    x = x_ref[...] + y
    xo_ref[...] = x
    hn = (x * lax.rsqrt(jnp.mean(x * x, axis=-1, keepdims=True) + EPS) * fn_ref[...]).astype(BF16)
    lg = jnp.dot(hn, wr_ref[...], preferred_element_type=F32) + br_ref[...]
    lane = lax.broadcasted_iota(jnp.int32, lg.shape, 1).astype(F32)
    neg = -jnp.inf
    gl = jnp.where(lane < N_GROUPS, lg, neg)
    gmax = jnp.max(gl, axis=1, keepdims=True)
    g_w = 1.0 / jnp.sum(jnp.exp(gl - gmax), axis=1, keepdims=True)
    g_idx = jnp.min(jnp.where(gl == gmax, lane, float(LANES)), axis=1, keepdims=True)
    e_lo = N_GROUPS + EXPERTS_PER_GROUP * g_idx
    el = jnp.where((lane >= e_lo) & (lane < e_lo + EXPERTS_PER_GROUP), lg, neg)
    max1 = jnp.max(el, axis=1, keepdims=True)
    e1 = jnp.min(jnp.where(el == max1, lane, float(LANES)), axis=1, keepdims=True)
    el2 = jnp.where(lane == e1, neg, el)
    max2 = jnp.max(el2, axis=1, keepdims=True)
    e2 = jnp.min(jnp.where(el2 == max2, lane, float(LANES)), axis=1, keepdims=True)
    p2 = jnp.exp(max2 - max1)
    w1 = g_w / (1.0 + p2)
    w2 = g_w * p2 / (1.0 + p2)
    route = jnp.where(lane == 0, w1, jnp.where(lane == 1, w2, jnp.where(
        lane == 2, e1 - N_GROUPS, jnp.where(lane == 3, e2 - N_GROUPS, 0.0))))
    route_ref[...] = route


def out_proj(a, m, w_a, w_m, x, ffn_norm, w_router, b_router, tm):
    t, d = x.shape
    ka, km = a.shape[1], m.shape[1]
    return pl.pallas_call(
        _out_proj_kernel, grid=(t // tm,),
        in_specs=[pl.BlockSpec((tm, ka), lambda i: (i, 0)), pl.BlockSpec((tm, km), lambda i: (i, 0)),
                  pl.BlockSpec((ka, d), lambda i: (0, 0)), pl.BlockSpec((km, d), lambda i: (0, 0)),
                  pl.BlockSpec((tm, d), lambda i: (i, 0)), pl.BlockSpec((1, d), lambda i: (0, 0)),
                  pl.BlockSpec((d, LANES), lambda i: (0, 0)), pl.BlockSpec((1, LANES), lambda i: (0, 0))],
        out_specs=[pl.BlockSpec((tm, d), lambda i: (i, 0)), pl.BlockSpec((tm, LANES), lambda i: (i, 0))],
        out_shape=[jax.ShapeDtypeStruct((t, d), F32), jax.ShapeDtypeStruct((t, LANES), F32)],
        compiler_params=_params("parallel"), name="out_proj")(a, m, w_a, w_m, x, ffn_norm, w_router, b_router)


def _moe_kernel(blk_exp_ref, tok_ref, dst_ref, nvalid_ref, nused_ref,
                x_hbm, fn_ref, wt_ref, wg_ref, wu_ref, wd_ref, out_hbm, xbuf, ybuf, gsem, ssem):
    del blk_exp_ref
    bm = MOE_BLOCK
    i = pl.program_id(0)
    n_used = nused_ref[0]

    def row_gather(blk, slot, r):
        return pltpu.make_async_copy(x_hbm.at[pl.ds(tok_ref[blk * bm + r], 1)], xbuf.at[slot, pl.ds(r, 1)],
                                     gsem.at[slot])

    def row_scatter(blk, r):
        return pltpu.make_async_copy(ybuf.at[pl.ds(r, 1)], out_hbm.at[pl.ds(dst_ref[blk * bm + r], 1)], ssem.at[0])

    def start_rows(copy, n):
        groups = n // MOE_DMA_UNROLL

        def group(g, c):
            for u in range(MOE_DMA_UNROLL):
                copy(g * MOE_DMA_UNROLL + u).start()
            return c

        lax.fori_loop(0, groups, group, 0)
        if not isinstance(n, int) or n % MOE_DMA_UNROLL:
            lax.fori_loop(groups * MOE_DMA_UNROLL, n, lambda r, c: (copy(r).start(), c)[1], 0)

    def start_gather(blk, slot):
        start_rows(functools.partial(row_gather, blk, slot), bm)

    def wait_gather(blk, slot):
        del blk
        pltpu.make_async_copy(x_hbm.at[pl.ds(0, bm)], xbuf.at[slot], gsem.at[slot]).wait()

    def start_scatter(blk):
        n = nvalid_ref[blk]

        @pl.when(n == bm)
        def _():
            for r in range(bm):
                row_scatter(blk, r).start()

        @pl.when(n < bm)
        def _():
            start_rows(functools.partial(row_scatter, blk), n)

    def wait_scatter(blk):
        n = nvalid_ref[blk]
        n8 = pl.multiple_of(n // 8 * 8, 8)

        @pl.when(n8 > 0)
        def _():
            pltpu.make_async_copy(ybuf.at[pl.ds(0, n8)], out_hbm.at[pl.ds(0, n8)], ssem.at[0]).wait()

        lax.fori_loop(n8, n, lambda r, c: (row_scatter(blk, r).wait(), c)[1], 0)

    @pl.when(i == 0)
    def _():
        start_gather(0, 0)

    @pl.when(i < n_used)
    def _():
        slot = i % 2
        wait_gather(i, slot)
        x = xbuf[slot]
        xn = (x * lax.rsqrt(jnp.mean(x * x, axis=-1, keepdims=True) + EPS) * fn_ref[...]).astype(BF16)
        for r in range(bm):
            row_gather(i + 1, 1 - slot, r).start()
        gate = jnp.dot(xn, wg_ref[...], preferred_element_type=F32)
        up = jnp.dot(xn, wu_ref[...], preferred_element_type=F32)
        act = (gate * jax.nn.sigmoid(gate) * up).astype(BF16)
        y = jnp.dot(act, wd_ref[...], preferred_element_type=F32) * wt_ref[...]

        @pl.when(i > 0)
        def _():
            wait_scatter(i - 1)

        ybuf[...] = y
        start_scatter(i)

        @pl.when(i == n_used - 1)
        def _():
            wait_scatter(i)
            wait_gather(i + 1, 1 - slot)


def moe_experts(x, ffn_norm, tables, w_gate, w_up, w_down, layer):
    bm = MOE_BLOCK
    t, d = x.shape
    f = w_gate.shape[3]
    tok_buf, dst_buf, wt_buf, blk_exp, n_valid, n_used = tables
    n_blocks = blk_exp.shape[0]
    const = lambda i, *_: (0, 0)
    expert = lambda i, be, *_: (layer, be[i], 0, 0)
    grid_spec = pltpu.PrefetchScalarGridSpec(
        num_scalar_prefetch=5, grid=(n_blocks,),
        in_specs=[pl.BlockSpec(memory_space=pl.ANY),
                  pl.BlockSpec((1, d), const),
                  pl.BlockSpec((bm, 1), lambda i, *_: (i, 0)),
                  pl.BlockSpec((None, None, d, f), expert),
                  pl.BlockSpec((None, None, d, f), expert),
                  pl.BlockSpec((None, None, f, d), expert)],
        out_specs=pl.BlockSpec(memory_space=pl.ANY),
        scratch_shapes=[pltpu.VMEM((2, bm, d), F32), pltpu.VMEM((bm, d), F32),
                        pltpu.SemaphoreType.DMA((2,)), pltpu.SemaphoreType.DMA((1,))])
    return pl.pallas_call(
        _moe_kernel, grid_spec=grid_spec, out_shape=jax.ShapeDtypeStruct((TOP_K * t, d), F32),
        compiler_params=_params("arbitrary"), name="moe_experts")(
            blk_exp, tok_buf, dst_buf, n_valid, n_used, x, ffn_norm, wt_buf, w_gate, w_up, w_down)


def moe_tables(route, n_blocks):
    bm = MOE_BLOCK
    t = route.shape[0]
    a = TOP_K * t
    eid = route[:, 2:4].astype(jnp.int32).T.reshape(a)
    wts = route[:, 0:2].T.reshape(a)
    shift = max(a - 1, 1).bit_length()
    key = jnp.sort(eid * (1 << shift) + jnp.arange(a, dtype=jnp.int32))
    order = key & ((1 << shift) - 1)
    experts = jnp.arange(N_EXPERTS, dtype=jnp.int32)
    start = jnp.sum((key[None, :] < (experts * (1 << shift))[:, None]).astype(jnp.int32), axis=1)
    counts = jnp.concatenate([start[1:], jnp.array([a], jnp.int32)]) - start
    padded = (counts + bm - 1) // bm * bm
    pend = jnp.cumsum(padded)
    pstart = pend - padded
    p = (n_blocks + 1) * bm
    blk_start = jnp.arange(n_blocks, dtype=jnp.int32) * bm
    blk_exp = jnp.minimum(jnp.sum((pend[None, :] <= blk_start[:, None]).astype(jnp.int32), axis=1), N_EXPERTS - 1)
    n_valid = jnp.clip(counts[blk_exp] - (blk_start - pstart[blk_exp]), 0, bm).astype(jnp.int32)
    n_used = (pend[-1] // bm).astype(jnp.int32).reshape(1)
    slot = jnp.arange(p, dtype=jnp.int32)
    slot_exp = blk_exp[jnp.minimum(slot // bm, n_blocks - 1)]
    rank = slot - pstart[slot_exp]
    valid = rank < counts[slot_exp]
    src = order[jnp.clip(start[slot_exp] + rank, 0, a - 1)]
    tok_buf = jnp.where(valid, src % t, 0)
    wt_buf = jnp.where(valid, wts[src], 0.0)
    dst_buf = jnp.where(valid, src, 0)
    return tok_buf, dst_buf, wt_buf.reshape(p, 1), blk_exp, n_valid, n_used


def moe_layer(x_mid, route, ffn_norm, w_gate, w_up, w_down, layer):
    t = x_mid.shape[0]
    a = TOP_K * t
    n_blocks = -(-a // MOE_BLOCK) + N_EXPERTS
    return moe_experts(x_mid, ffn_norm, moe_tables(route, n_blocks), w_gate, w_up, w_down, layer)


DIFF_TQ = 512
DIFF_KS = 256
DIFF_CW = 256
DIFF_AHEAD = 4
DIFF_ONES = 16


def _diff_attn_kernel(q_ref, k_ref, v_ref, lam_ref, subln_ref, out_ref, vt_ref, qst_ref, m_ref, acc_ref, s_ref,
                      *, tq, ks, cw, ahead, lambda_init):
    hd = DIFF_HD
    dv = DIFF_DV
    seq = k_ref.shape[0]
    qi = pl.program_id(2)
    n_cols = 2 * tq

    @pl.when(qi == 0)
    def _():
        for j in range(seq // tq):
            vt_ref[0:dv, j * tq:(j + 1) * tq] = v_ref[j * tq:(j + 1) * tq, :].astype(F32).T.astype(BF16)
        vt_ref[dv:dv + DIFF_ONES, :] = jnp.ones((DIFF_ONES, seq), BF16)

    qt = q_ref[...].astype(F32).T
    row = lax.broadcasted_iota(jnp.int32, qt.shape, 0)
    qst_ref[:, 0:tq] = jnp.where(row < hd, qt, 0.0).astype(BF16)
    qst_ref[:, tq:n_cols] = jnp.where(row >= hd, qt, 0.0).astype(BF16)
    m_ref[...] = jnp.full_like(m_ref, -jnp.inf)
    acc_ref[...] = jnp.zeros_like(acc_ref)

    assert ahead == n_cols // cw and ks < tq, "the carried pieces are key piece 0 of every column chunk"

    def scores(off, kc, c):
        k = k_ref[pl.ds(off + kc * ks, ks), :]
        return jnp.dot(k, qst_ref[:, c * cw:(c + 1) * cw], preferred_element_type=F32)

    for c in range(ahead):
        s_ref[c] = scores(0, 0, c)

    def tile(ki, diagonal):
        off = pl.multiple_of(ki * tq, tq)
        items = []
        for kc in range(tq // ks):
            for c in range(n_cols // cw):
                q_lo = (c * cw) % tq
                if diagonal and q_lo + cw <= kc * ks:
                    continue
                items.append((kc, c, diagonal and (kc + 1) * ks - 1 > q_lo))

        pending = {}
        for n, (kc, c, masked) in enumerate(items):
            s = s_ref[n] if n < ahead else pending.pop(n)
            nxt = n + ahead
            if nxt < len(items):
                pending[nxt] = scores(off, *items[nxt][:2])
            elif not diagonal:
                s_ref[nxt - len(items)] = scores(off + tq, 0, nxt - len(items))
            cols = slice(c * cw, (c + 1) * cw)
            if masked:
                kpos = kc * ks + lax.broadcasted_iota(jnp.int32, s.shape, 0)
                qpos = (c * cw) % tq + lax.broadcasted_iota(jnp.int32, s.shape, 1)
                s = jnp.where(kpos <= qpos, s, -jnp.inf)
            m_prev = m_ref[:, cols]
            m_new = jnp.maximum(m_prev, jnp.max(s, axis=0, keepdims=True))
            alpha = jnp.exp2(m_prev - m_new)
            p = jnp.exp2(s - m_new)
            vt = vt_ref[:, pl.ds(off + kc * ks, ks)]
            acc_ref[:, cols] = alpha * acc_ref[:, cols] + jnp.dot(vt, p.astype(BF16), preferred_element_type=F32)
            m_ref[:, cols] = m_new

    lax.fori_loop(0, qi, lambda ki, c: (tile(ki, False), c)[1], 0)
    tile(qi, True)

    lv = lam_ref[...]
    lam = (jnp.exp(jnp.sum(lv[0:1] * lv[1:2], axis=1, keepdims=True))
           - jnp.exp(jnp.sum(lv[2:3] * lv[3:4], axis=1, keepdims=True)) + lambda_init)
    o = (acc_ref[0:dv, 0:tq] / acc_ref[dv:dv + 1, 0:tq]
         - lam * (acc_ref[0:dv, tq:n_cols] / acc_ref[dv:dv + 1, tq:n_cols]))
    o = o * lax.rsqrt(jnp.mean(o * o, axis=0, keepdims=True) + EPS) * subln_ref[...] * (1.0 - lambda_init)
    out_ref[...] = o.T.astype(out_ref.dtype)


def diff_attn(proj, lam_vecs, subln_col, lambda_init, batch, seq, tq):
    nh = DIFF_HEADS
    nq = seq // tq
    kern = functools.partial(_diff_attn_kernel, tq=tq, ks=min(DIFF_KS, tq), cw=min(DIFF_CW, tq), ahead=DIFF_AHEAD,
                             lambda_init=lambda_init)
    return pl.pallas_call(
        kern, grid=(batch, nh, nq),
        in_specs=[pl.BlockSpec((tq, LANES), lambda b, h, i: (b * nq + i, 2 * nh + h)),
                  pl.BlockSpec((seq, LANES), lambda b, h, i: (b, h)),
                  pl.BlockSpec((seq, LANES), lambda b, h, i: (b, nh + h)),
                  pl.BlockSpec((4, DIFF_HD), lambda b, h, i: (0, 0)),
                  pl.BlockSpec((DIFF_DV, 1), lambda b, h, i: (0, 0))],
        out_specs=pl.BlockSpec((tq, DIFF_DV), lambda b, h, i: (b * nq + i, h)),
        out_shape=jax.ShapeDtypeStruct((batch * seq, nh * DIFF_DV), BF16),
        scratch_shapes=[pltpu.VMEM((DIFF_DV + DIFF_ONES, seq), BF16), pltpu.VMEM((LANES, 2 * tq), BF16),
                        pltpu.VMEM((1, 2 * tq), F32), pltpu.VMEM((DIFF_DV + DIFF_ONES, 2 * tq), F32),
                        pltpu.VMEM((DIFF_AHEAD, min(DIFF_KS, tq), min(DIFF_CW, tq)), F32)],
        compiler_params=_params("parallel", "parallel", "arbitrary"), name="diff_attn")(
            proj, proj, proj, lam_vecs, subln_col)


def _combine_kernel(x_ref, a_ref, b_ref, out_ref):
    out_ref[...] = x_ref[...] + a_ref[...] + b_ref[...]


def combine(x_mid, moe_out, tm):
    t, d = x_mid.shape
    nb = t // tm
    return pl.pallas_call(
        _combine_kernel, grid=(nb,),
        in_specs=[pl.BlockSpec((tm, d), lambda i: (i, 0)), pl.BlockSpec((tm, d), lambda i: (i, 0)),
                  pl.BlockSpec((tm, d), lambda i: (i + nb, 0))],
        out_specs=pl.BlockSpec((tm, d), lambda i: (i, 0)),
        out_shape=jax.ShapeDtypeStruct((t, d), F32),
        compiler_params=_params("parallel"), name="combine")(x_mid, moe_out, moe_out)


def _row_tile(n, want):
    while n % want:
        want //= 2
    return want


def _router_weights(w_group, b_group, w_expert, b_expert):
    d = w_group.shape[0]
    pad = LANES - N_GROUPS - N_EXPERTS
    w = jnp.concatenate([w_group, w_expert, jnp.zeros((d, pad), F32)], axis=1).astype(BF16)
    b = jnp.concatenate([b_group, b_expert, jnp.zeros((pad,), F32)]).reshape(1, LANES)
    return w, b


def kernel(x, mem, positions, attn_norm, mem_norm, mem_w_kv, mem_qnorm, mem_knorm, a_w_in, a_gate_bias,
           a_head_norm, a_w_out, kv_norm, kv_w, kv_knorm, b_w_in, b_qnorm, b_lambda, b_subln, b_w_out, ffn_norm,
           moe_w_group, moe_b_group, moe_w_expert, moe_b_expert, moe_w_gate, moe_w_up, moe_w_down):
    batch, seq, d = x.shape
    slots = mem.shape[1]
    t = batch * seq
    tm = _row_tile(t, 512)
    a_qk = MLSTM_HEADS * MLSTM_DQK
    a_v = MLSTM_HEADS * MLSTM_DV
    mem_w = MEM_HEADS * MEM_HEAD_DIM
    diff_qk = 2 * DIFF_HEADS * DIFF_HD
    xt = x.reshape(t, d)
    w_gate, w_up, w_down = moe_w_gate.astype(BF16), moe_w_up.astype(BF16), moe_w_down.astype(BF16)

    memt = mem.reshape(batch * slots, d)
    w_mkv = jnp.concatenate([mem_w_kv[0], mem_w_kv[1]], axis=1).astype(BF16)
    (mkv,) = norm_proj([memt], [0], batch * slots, mem_norm, w_mkv, tm=_row_tile(batch * slots, 512), tn=512,
                       norm_bounds=(2 * mem_w // 512,))

    g_lo = 2 * a_qk + 2 * a_v
    w_in = a_w_in[0]
    w_main = jnp.concatenate([w_in[:, :g_lo], w_in[:, g_lo + 2 * MLSTM_HEADS:]], axis=1).astype(BF16)
    w_gates = jnp.pad(w_in[:, g_lo:g_lo + 2 * MLSTM_HEADS], ((0, 0), (0, LANES - 2 * MLSTM_HEADS))).astype(BF16)
    proj0, gates = norm_proj([xt], [0], t, attn_norm[0:1], w_main, tm=tm, tn=512, w_aux=w_gates)
    gate_bias = jnp.pad(a_gate_bias[0], (0, LANES - 2 * MLSTM_HEADS)).reshape(1, LANES)
    hm = mlstm(proj0, gates, gate_bias, a_head_norm[0], batch, seq)
    mo0 = mem_attn(proj0, g_lo // mem_w, mkv, 0, mem_qnorm[0:1], mem_knorm[0:1], batch, seq, slots, _row_tile(seq, 512))
    w_r0, b_r0 = _router_weights(moe_w_group[0], moe_b_group[0], moe_w_expert[0], moe_b_expert[0])
    w_out0 = a_w_out[0].astype(BF16)
    x_mid0, route0 = out_proj(hm, mo0, w_out0[:a_v], w_out0[a_v:], xt, ffn_norm[0:1], w_r0, b_r0, tm)
    moe0 = moe_layer(x_mid0, route0, ffn_norm[0:1], w_gate, w_up, w_down, 0)

    half = DIFF_HD // 2
    lane = jnp.arange(LANES)
    inv_freq = ROPE_THETA ** (-(lane % half).astype(F32) / half)
    ang = positions.astype(F32).reshape(t, 1) * inv_freq[None, :]
    cos = jnp.cos(ang)
    sin = jnp.sin(ang) * jnp.where((lane % DIFF_HD) < half, -1.0, 1.0)[None, :]
    tn = 512
    w1 = jnp.concatenate([kv_w, b_w_in[0]], axis=1).astype(BF16)
    n_tiles = w1.shape[1] // tn
    k_tiles = diff_qk // tn
    q_lo = kv_w.shape[1] // tn
    reps = LANES // DIFF_HD
    ew = jnp.zeros((n_tiles, 1, LANES), F32)
    ew = ew.at[0:k_tiles].set(jnp.tile(kv_knorm, reps)[None, None, :])
    q_scale = DIFF_HD ** -0.5 * math.log2(math.e)
    ew = ew.at[q_lo:q_lo + k_tiles].set(jnp.tile(b_qnorm[0], reps)[None, None, :] * q_scale)
    norms1 = jnp.stack([kv_norm, attn_norm[1]])
    nb = t // tm
    proj1, x1 = norm_proj([x_mid0, moe0, moe0], [0, 0, nb], t, norms1, w1, tm=tm, tn=tn, norm_bounds=(q_lo,),
                          rope=(cos, sin, ew), rope_ranges=((0, k_tiles), (q_lo, q_lo + k_tiles)), write_sum=True)

    lambda_init = 0.8 - 0.6 * math.exp(-0.3 * 1)
    oa = diff_attn(proj1, b_lambda[0], b_subln[0].reshape(DIFF_DV, 1), lambda_init, batch, seq,
                   _row_tile(seq, DIFF_TQ))
    mo1 = mem_attn(proj1, (kv_w.shape[1] + diff_qk) // mem_w, mkv, 1, mem_qnorm[1:2], mem_knorm[1:2], batch, seq,
                   slots, _row_tile(seq, 512))
    w_r1, b_r1 = _router_weights(moe_w_group[1], moe_b_group[1], moe_w_expert[1], moe_b_expert[1])
    w_out1 = b_w_out[0].astype(BF16)
    n_oa = DIFF_HEADS * DIFF_DV
    x_mid1, route1 = out_proj(oa, mo1, w_out1[:n_oa], w_out1[n_oa:], x1, ffn_norm[1:2], w_r1, b_r1, tm)
    moe1 = moe_layer(x_mid1, route1, ffn_norm[1:2], w_gate, w_up, w_down, 1)
    out = combine(x_mid1, moe1, tm)
    return out.reshape(batch, seq, d)
```

```python
import functools
import math

import jax
import jax.numpy as jnp
from jax import lax
from jax.experimental import pallas as pl
from jax.experimental.pallas import tpu as pltpu

F32 = jnp.float32
BF16 = jnp.bfloat16
EPS = 1e-6
LANES = 128
VMEM_LIMIT_BYTES = 56 * 1024 * 1024

MLSTM_HEADS = 6
MLSTM_DQK = 128
MLSTM_DV = 256
MLSTM_CHUNK = 128
MEM_HEADS = 4
MEM_HEAD_DIM = 128
DIFF_HEADS = 12
DIFF_HD = 64
DIFF_DV = 128
ROPE_THETA = 10000.0
N_GROUPS = 4
EXPERTS_PER_GROUP = 8
N_EXPERTS = N_GROUPS * EXPERTS_PER_GROUP
TOP_K = 2
MOE_BLOCK = 256


def _params(*semantics):
    return pltpu.CompilerParams(dimension_semantics=semantics, vmem_limit_bytes=VMEM_LIMIT_BYTES)


def _norm_proj_kernel(*refs, norm_bounds, rope_ranges, has_aux, hd):
    it = iter(refs)
    x_ref = next(it)
    normw_ref = next(it)
    w_ref = next(it)
    if rope_ranges:
        cos_ref, sin_ref, ew_ref = next(it), next(it), next(it)
    if has_aux:
        waux_ref = next(it)
    out_ref = next(it)
    if has_aux:
        aux_ref = next(it)
    h_ref = next(it)

    j = pl.program_id(1)
    n_norm = len(norm_bounds) + 1

    @pl.when(j == 0)
    def _():
        x = x_ref[...]
        xn = x * lax.rsqrt(jnp.mean(x * x, axis=-1, keepdims=True) + EPS)
        for n in range(n_norm):
            h_ref[n] = (xn * normw_ref[n:n + 1, :]).astype(BF16)
        if has_aux:
            aux_ref[...] = jnp.dot(h_ref[0], waux_ref[...], preferred_element_type=F32)

    sel = jnp.int32(0)
    for b in norm_bounds:
        sel = sel + (j >= b).astype(jnp.int32)
    acc = jnp.dot(h_ref[sel], w_ref[...], preferred_element_type=F32)

    if not rope_ranges:
        out_ref[...] = acc.astype(out_ref.dtype)
        return

    is_rope = jnp.bool_(False)
    for lo, hi in rope_ranges:
        is_rope = is_rope | ((j >= lo) & (j < hi))

    @pl.when(is_rope)
    def _():
        ew = ew_ref[j]
        cs = cos_ref[...]
        sn = sin_ref[...]
        r = lax.broadcasted_iota(jnp.int32, (2 * LANES, LANES), 0)
        c = lax.broadcasted_iota(jnp.int32, (2 * LANES, LANES), 1)
        seg = ((r % LANES) // hd == c // hd).astype(BF16)
        lane = lax.broadcasted_iota(jnp.int32, (acc.shape[0], LANES), 1)
        first_half = (lane % hd) < (hd // 2)
        for s in range(acc.shape[1] // LANES):
            t = acc[:, s * LANES:(s + 1) * LANES]
            tt = t * t
            hi_part = tt.astype(BF16)
            lo_part = (tt - hi_part.astype(F32)).astype(BF16)
            ss = jnp.dot(jnp.concatenate([hi_part, lo_part], axis=1), seg, preferred_element_type=F32)
            y = t * lax.rsqrt(ss * (1.0 / hd) + EPS) * ew
            rot = jnp.where(first_half, pltpu.roll(y, LANES - hd // 2, 1), pltpu.roll(y, hd // 2, 1))
            out_ref[:, s * LANES:(s + 1) * LANES] = (y * cs + rot * sn).astype(out_ref.dtype)

    @pl.when(jnp.logical_not(is_rope))
    def _():
        out_ref[...] = acc.astype(out_ref.dtype)


def norm_proj(x, norm_w, w, *, tm, tn, norm_bounds=(), rope=None, rope_ranges=(), w_aux=None):
    n_rows, d = x.shape
    n = w.shape[1]
    n_norm = norm_w.shape[0]
    grid = (n_rows // tm, n // tn)
    in_specs = [pl.BlockSpec((tm, d), lambda i, j: (i, 0)), pl.BlockSpec((n_norm, d), lambda i, j: (0, 0)),
                pl.BlockSpec((d, tn), lambda i, j: (0, j))]
    args = [x, norm_w, w]
    if rope_ranges:
        cos, sin, ew = rope
        in_specs += [pl.BlockSpec((tm, LANES), lambda i, j: (i, 0)), pl.BlockSpec((tm, LANES), lambda i, j: (i, 0)),
                     pl.BlockSpec(ew.shape, lambda i, j: (0, 0, 0))]
        args += [cos, sin, ew]
    if w_aux is not None:
        in_specs.append(pl.BlockSpec((d, LANES), lambda i, j: (0, 0)))
        args.append(w_aux)
    out_shape = [jax.ShapeDtypeStruct((n_rows, n), BF16)]
    out_specs = [pl.BlockSpec((tm, tn), lambda i, j: (i, j))]
    if w_aux is not None:
        out_shape.append(jax.ShapeDtypeStruct((n_rows, LANES), F32))
        out_specs.append(pl.BlockSpec((tm, LANES), lambda i, j: (i, 0)))
    kern = functools.partial(_norm_proj_kernel, norm_bounds=tuple(norm_bounds), rope_ranges=tuple(rope_ranges),
                             has_aux=w_aux is not None, hd=DIFF_HD)
    return pl.pallas_call(
        kern, grid=grid, in_specs=in_specs, out_specs=out_specs, out_shape=out_shape,
        scratch_shapes=[pltpu.VMEM((n_norm, tm, d), BF16)],
        compiler_params=_params("parallel", "arbitrary"), name="norm_proj")(*args)


def _log_sigmoid(x):
    return jnp.minimum(x, 0.0) - jnp.log1p(jnp.exp(-jnp.abs(x)))


def _mlstm_kernel(q_ref, k_ref, v_ref, o_ref, g_ref, bias_ref, hnorm_ref, out_ref, c_ref, n_ref, m_ref):
    L, H, DQK, DV = MLSTM_CHUNK, MLSTM_HEADS, MLSTM_DQK, MLSTM_DV
    scale = DQK ** -0.5

    @pl.when(pl.program_id(1) == 0)
    def _():
        c_ref[...] = jnp.zeros_like(c_ref)
        n_ref[...] = jnp.zeros_like(n_ref)
        m_ref[...] = jnp.zeros_like(m_ref)

    g = g_ref[...] + bias_ref[...]
    ls = _log_sigmoid(g)
    g_t = g.T
    ls_t = ls.T
    row = lax.broadcasted_iota(jnp.int32, (L, L), 0)
    col = lax.broadcasted_iota(jnp.int32, (L, L), 1)
    causal = col <= row

    for h in range(H):
        q = q_ref[:, h * DQK:(h + 1) * DQK]
        k = k_ref[:, h * DQK:(h + 1) * DQK]
        v = v_ref[:, h * DV:(h + 1) * DV]
        i_col = g[:, h:h + 1]
        i_row = g_t[h:h + 1, :]
        lf_col = ls[:, H + h:H + h + 1]
        lf_row = ls_t[H + h:H + h + 1, :]
        b_col = jnp.sum(jnp.where(causal, lf_row, 0.0), axis=1, keepdims=True)
        b_row = jnp.sum(jnp.where(row <= col, lf_col, 0.0), axis=0, keepdims=True)
        m_prev = m_ref[h][:, 0:1]
        c_prev = c_ref[h]
        n_prev = n_ref[h]

        d_log = jnp.where(causal, b_col - b_row + i_row, -jnp.inf)
        inter = b_col + m_prev
        m_t = jnp.maximum(inter, jnp.max(d_log, axis=1, keepdims=True))
        w_intra = jnp.exp(d_log - m_t)
        w_inter = jnp.exp(inter - m_t) * scale
        qk = lax.dot_general(q, k, (((1,), (1,)), ((), ())), preferred_element_type=F32)
        s = qk * scale * w_intra
        num = jnp.dot(s.astype(BF16), v, preferred_element_type=F32)
        num = num + w_inter * jnp.dot(q, c_prev.astype(BF16), preferred_element_type=F32)
        den = jnp.sum(s, axis=1, keepdims=True) + w_inter * jnp.sum(q.astype(F32) * n_prev, axis=1, keepdims=True)
        hout = num / jnp.maximum(jnp.abs(den), jnp.exp(-m_t))

        b_last = b_row[:, L - 1:L]
        st_row = b_last - b_row + i_row
        m_new = jnp.maximum(b_last + m_prev, jnp.max(st_row, axis=1, keepdims=True))
        w_st_col = jnp.exp(b_last - b_col + i_col - m_new)
        decay = jnp.exp(b_last + m_prev - m_new)
        kw = k.astype(F32) * w_st_col
        c_ref[h] = decay * c_prev + lax.dot_general(kw.astype(BF16), v, (((0,), (0,)), ((), ())),
                                                    preferred_element_type=F32)
        n_ref[h] = decay * n_prev + jnp.sum(kw, axis=0, keepdims=True)
        m_ref[h] = jnp.broadcast_to(m_new, (1, LANES))

        hn = hout * lax.rsqrt(jnp.mean(hout * hout, axis=-1, keepdims=True) + EPS) * hnorm_ref[h:h + 1, :]
        og = o_ref[:, h * DV:(h + 1) * DV].astype(F32)
        out_ref[:, h * DV:(h + 1) * DV] = (hn * jax.nn.sigmoid(og)).astype(out_ref.dtype)


def mlstm(proj, gates, gate_bias, head_norm, batch, seq):
    L, H, DQK, DV = MLSTM_CHUNK, MLSTM_HEADS, MLSTM_DQK, MLSTM_DV
    nc = seq // L
    qk_w, v_w = H * DQK, H * DV
    row = lambda b, c: b * nc + c
    return pl.pallas_call(
        _mlstm_kernel, grid=(batch, nc),
        in_specs=[pl.BlockSpec((L, qk_w), lambda b, c: (row(b, c), 0)),
                  pl.BlockSpec((L, qk_w), lambda b, c: (row(b, c), 1)),
                  pl.BlockSpec((L, v_w), lambda b, c: (row(b, c), 1)),
                  pl.BlockSpec((L, v_w), lambda b, c: (row(b, c), 2)),
                  pl.BlockSpec((L, LANES), lambda b, c: (row(b, c), 0)),
                  pl.BlockSpec((1, LANES), lambda b, c: (0, 0)),
                  pl.BlockSpec((H, DV), lambda b, c: (0, 0))],
        out_specs=pl.BlockSpec((L, v_w), lambda b, c: (row(b, c), 0)),
        out_shape=jax.ShapeDtypeStruct((batch * seq, v_w), BF16),
        scratch_shapes=[pltpu.VMEM((H, DQK, DV), F32), pltpu.VMEM((H, 1, DQK), F32), pltpu.VMEM((H, 1, LANES), F32)],
        compiler_params=_params("parallel", "arbitrary"), name="mlstm")(
            proj, proj, proj, proj, gates, gate_bias, head_norm)


def _mem_attn_kernel(q_ref, k_ref, v_ref, qn_ref, kn_ref, out_ref):
    hd = MEM_HEAD_DIM
    scale = hd ** -0.5
    for h in range(MEM_HEADS):
        sl = slice(h * hd, (h + 1) * hd)
        q = q_ref[:, sl].astype(F32)
        q = q * lax.rsqrt(jnp.mean(q * q, axis=-1, keepdims=True) + EPS) * qn_ref[...]
        k = k_ref[:, sl].astype(F32)
        k = k * lax.rsqrt(jnp.mean(k * k, axis=-1, keepdims=True) + EPS) * kn_ref[...]
        s = lax.dot_general(q.astype(BF16), k.astype(BF16), (((1,), (1,)), ((), ())),
                            preferred_element_type=F32) * scale
        p = jnp.exp(s - jnp.max(s, axis=-1, keepdims=True))
        p = p / jnp.sum(p, axis=-1, keepdims=True)
        out_ref[:, sl] = jnp.dot(p.astype(BF16), v_ref[:, sl], preferred_element_type=F32).astype(out_ref.dtype)


def mem_attn(proj, q_col_block, mkv, layer, qnorm, knorm, batch, seq, slots, tm):
    width = MEM_HEADS * MEM_HEAD_DIM
    nt = seq // tm
    return pl.pallas_call(
        _mem_attn_kernel, grid=(batch, nt),
        in_specs=[pl.BlockSpec((tm, width), lambda b, i: (b * nt + i, q_col_block)),
                  pl.BlockSpec((slots, width), lambda b, i: (b, 2 * layer)),
                  pl.BlockSpec((slots, width), lambda b, i: (b, 2 * layer + 1)),
                  pl.BlockSpec((1, MEM_HEAD_DIM), lambda b, i: (0, 0)),
                  pl.BlockSpec((1, MEM_HEAD_DIM), lambda b, i: (0, 0))],
        out_specs=pl.BlockSpec((tm, width), lambda b, i: (b * nt + i, 0)),
        out_shape=jax.ShapeDtypeStruct((batch * seq, width), BF16),
        compiler_params=_params("parallel", "parallel"), name="mem_attn")(proj, mkv, mkv, qnorm, knorm)


def _out_proj_kernel(a_ref, m_ref, wa_ref, wm_ref, x_ref, fn_ref, wr_ref, br_ref, xo_ref, route_ref, cnt_ref,
                     base_ref):
    @pl.when(pl.program_id(0) == 0)
    def _():
        base_ref[...] = jnp.zeros_like(base_ref)

    y = jnp.dot(a_ref[...], wa_ref[...], preferred_element_type=F32)
    y = y + jnp.dot(m_ref[...], wm_ref[...], preferred_element_type=F32)
    x = x_ref[...] + y
    xo_ref[...] = x
    hn = (x * lax.rsqrt(jnp.mean(x * x, axis=-1, keepdims=True) + EPS) * fn_ref[...]).astype(BF16)
    lg = jnp.dot(hn, wr_ref[...], preferred_element_type=F32) + br_ref[...]
    lane = lax.broadcasted_iota(jnp.int32, lg.shape, 1).astype(F32)
    neg = -jnp.inf
    gl = jnp.where(lane < N_GROUPS, lg, neg)
    gmax = jnp.max(gl, axis=1, keepdims=True)
    g_w = 1.0 / jnp.sum(jnp.exp(gl - gmax), axis=1, keepdims=True)
    g_idx = jnp.min(jnp.where(gl == gmax, lane, float(LANES)), axis=1, keepdims=True)
    e_lo = N_GROUPS + EXPERTS_PER_GROUP * g_idx
    el = jnp.where((lane >= e_lo) & (lane < e_lo + EXPERTS_PER_GROUP), lg, neg)
    max1 = jnp.max(el, axis=1, keepdims=True)
    e1 = jnp.min(jnp.where(el == max1, lane, float(LANES)), axis=1, keepdims=True)
    el2 = jnp.where(lane == e1, neg, el)
    max2 = jnp.max(el2, axis=1, keepdims=True)
    e2 = jnp.min(jnp.where(el2 == max2, lane, float(LANES)), axis=1, keepdims=True)
    p2 = jnp.exp(max2 - max1)
    w1 = g_w / (1.0 + p2)
    w2 = g_w * p2 / (1.0 + p2)
    hot = ((lane == e1) | (lane == e2)).astype(BF16)
    tm = hot.shape[0]
    r = lax.broadcasted_iota(jnp.int32, (tm, tm), 0)
    c = lax.broadcasted_iota(jnp.int32, (tm, tm), 1)
    before = jnp.dot((c < r).astype(BF16), hot, preferred_element_type=F32) + base_ref[0:1, :]
    rank1 = jnp.sum(jnp.where(lane == e1, before, 0.0), axis=1, keepdims=True)
    rank2 = jnp.sum(jnp.where(lane == e2, before, 0.0), axis=1, keepdims=True)
    total = base_ref[0:1, :] + jnp.sum(hot.astype(F32), axis=0, keepdims=True)
    base_ref[...] = jnp.broadcast_to(total, base_ref.shape)
    cnt_ref[...] = jnp.broadcast_to(total, cnt_ref.shape)
    vals = (w1, w2, e1 - N_GROUPS, e2 - N_GROUPS, rank1, rank2)
    route = jnp.zeros_like(lg)
    for n, v in enumerate(vals):
        route = jnp.where(lane == n, v, route)
    route_ref[...] = route


def out_proj(a, m, w_a, w_m, x, ffn_norm, w_router, b_router, tm):
    t, d = x.shape
    ka, km = a.shape[1], m.shape[1]
    return pl.pallas_call(
        _out_proj_kernel, grid=(t // tm,),
        in_specs=[pl.BlockSpec((tm, ka), lambda i: (i, 0)), pl.BlockSpec((tm, km), lambda i: (i, 0)),
                  pl.BlockSpec((ka, d), lambda i: (0, 0)), pl.BlockSpec((km, d), lambda i: (0, 0)),
                  pl.BlockSpec((tm, d), lambda i: (i, 0)), pl.BlockSpec((1, d), lambda i: (0, 0)),
                  pl.BlockSpec((d, LANES), lambda i: (0, 0)), pl.BlockSpec((1, LANES), lambda i: (0, 0))],
        out_specs=[pl.BlockSpec((tm, d), lambda i: (i, 0)), pl.BlockSpec((tm, LANES), lambda i: (i, 0)),
                   pl.BlockSpec((8, LANES), lambda i: (0, 0))],
        out_shape=[jax.ShapeDtypeStruct((t, d), F32), jax.ShapeDtypeStruct((t, LANES), F32),
                   jax.ShapeDtypeStruct((8, LANES), F32)],
        scratch_shapes=[pltpu.VMEM((8, LANES), F32)],
        compiler_params=_params("arbitrary"), name="out_proj")(a, m, w_a, w_m, x, ffn_norm, w_router, b_router)


def _dispatch_kernel(dest_ref, pad_lo_ref, pad_n_ref, x_hbm, xs_hbm, zero_ref, sem, zsem, *, tm, top_k):
    i = pl.program_id(0)
    n_rows = top_k * tm

    def row_copy(step, r):
        tok = step * tm + r % tm
        return pltpu.make_async_copy(x_hbm.at[pl.ds(tok, 1)], xs_hbm.at[pl.ds(dest_ref[step * n_rows + r], 1)],
                                     sem.at[step % 2])

    def wait_step(step):
        pltpu.make_async_copy(x_hbm.at[pl.ds(0, n_rows)], xs_hbm.at[pl.ds(0, n_rows)], sem.at[step % 2]).wait()

    def zero_copy(e, r):
        return pltpu.make_async_copy(zero_ref.at[pl.ds(0, 1)], xs_hbm.at[pl.ds(pad_lo_ref[e] + r, 1)], zsem.at[0])

    def zero_tail(g):
        lo = pl.multiple_of(pad_lo_ref[N_EXPERTS] + g * zero_ref.shape[0], zero_ref.shape[0])
        return pltpu.make_async_copy(zero_ref, xs_hbm.at[pl.ds(lo, zero_ref.shape[0])], zsem.at[0])

    @pl.when(i == 0)
    def _():
        zero_ref[...] = jnp.zeros_like(zero_ref)
        tail_groups = pad_n_ref[N_EXPERTS] // zero_ref.shape[0]
        for e in range(N_EXPERTS):
            lax.fori_loop(0, pad_n_ref[e], lambda r, c, e=e: (zero_copy(e, r).start(), c)[1], 0)
        lax.fori_loop(0, tail_groups, lambda g, c: (zero_tail(g).start(), c)[1], 0)
        for e in range(N_EXPERTS):
            lax.fori_loop(0, pad_n_ref[e], lambda r, c, e=e: (zero_copy(e, r).wait(), c)[1], 0)
        lax.fori_loop(0, tail_groups, lambda g, c: (zero_tail(g).wait(), c)[1], 0)

    for r in range(n_rows):
        row_copy(i, r).start()

    @pl.when(i > 0)
    def _():
        wait_step(i - 1)

    @pl.when(i == pl.num_programs(0) - 1)
    def _():
        wait_step(i)


def moe_dispatch(x, dest, pad_lo, pad_n, n_sorted_rows, tm):
    t, d = x.shape
    kern = functools.partial(_dispatch_kernel, tm=tm, top_k=TOP_K)
    grid_spec = pltpu.PrefetchScalarGridSpec(
        num_scalar_prefetch=3, grid=(t // tm,),
        in_specs=[pl.BlockSpec(memory_space=pl.ANY)], out_specs=pl.BlockSpec(memory_space=pl.ANY),
        scratch_shapes=[pltpu.VMEM((8, d), F32), pltpu.SemaphoreType.DMA((2,)), pltpu.SemaphoreType.DMA((1,))])
    return pl.pallas_call(kern, grid_spec=grid_spec, out_shape=jax.ShapeDtypeStruct((n_sorted_rows, d), F32),
                          compiler_params=_params("arbitrary"), name="moe_dispatch")(dest, pad_lo, pad_n, x)


def _moe_kernel(blk_exp_ref, nused_ref, xs_ref, fn_ref, wg_ref, wu_ref, wd_ref, ys_ref):
    del blk_exp_ref

    @pl.when(pl.program_id(0) < nused_ref[0])
    def _():
        x = xs_ref[...]
        xn = (x * lax.rsqrt(jnp.mean(x * x, axis=-1, keepdims=True) + EPS) * fn_ref[...]).astype(BF16)
        gate = jnp.dot(xn, wg_ref[...], preferred_element_type=F32)
        up = jnp.dot(xn, wu_ref[...], preferred_element_type=F32)
        act = (gate * jax.nn.sigmoid(gate) * up).astype(BF16)
        ys_ref[...] = jnp.dot(act, wd_ref[...], preferred_element_type=F32)

    @pl.when(pl.program_id(0) >= nused_ref[0])
    def _():
        ys_ref[...] = jnp.zeros_like(ys_ref)


def moe_experts(xs, ffn_norm, blk_exp, n_used, w_gate, w_up, w_down, layer):
    bm = MOE_BLOCK
    p, d = xs.shape
    f = w_gate.shape[3]
    expert = lambda i, be, nu: (layer, be[i], 0, 0)
    grid_spec = pltpu.PrefetchScalarGridSpec(
        num_scalar_prefetch=2, grid=(p // bm,),
        in_specs=[pl.BlockSpec((bm, d), lambda i, be, nu: (jnp.minimum(i, nu[0] - 1), 0)),
                  pl.BlockSpec((1, d), lambda i, be, nu: (0, 0)),
                  pl.BlockSpec((None, None, d, f), expert),
                  pl.BlockSpec((None, None, d, f), expert),
                  pl.BlockSpec((None, None, f, d), expert)],
        out_specs=pl.BlockSpec((bm, d), lambda i, be, nu: (i, 0)))
    return pl.pallas_call(
        _moe_kernel, grid_spec=grid_spec, out_shape=jax.ShapeDtypeStruct((p, d), F32),
        compiler_params=_params("arbitrary"), name="moe_experts")(blk_exp, n_used, xs, ffn_norm, w_gate, w_up, w_down)


def _combine_kernel(dest_ref, x_ref, route_ref, ys_hbm, out_ref, ybuf, sem, *, tm, top_k):
    i = pl.program_id(0)
    n_rows = top_k * tm
    slot = i % 2

    def row_gather(step, slot, r):
        return pltpu.make_async_copy(ys_hbm.at[pl.ds(dest_ref[step * n_rows + r], 1)], ybuf.at[slot, pl.ds(r, 1)],
                                     sem.at[slot])

    def wait_gather(slot):
        pltpu.make_async_copy(ys_hbm.at[pl.ds(0, n_rows)], ybuf.at[slot], sem.at[slot]).wait()

    @pl.when(i == 0)
    def _():
        for r in range(n_rows):
            row_gather(0, 0, r).start()

    for r in range(n_rows):
        row_gather(i + 1, 1 - slot, r).start()
    wait_gather(slot)
    out = x_ref[...]
    for j in range(top_k):
        out = out + route_ref[:, j:j + 1] * ybuf[slot, j * tm:(j + 1) * tm, :]
    out_ref[...] = out

    @pl.when(i == pl.num_programs(0) - 1)
    def _():
        wait_gather(1 - slot)


def moe_combine(x, route, ys, dest, tm):
    t, d = x.shape
    kern = functools.partial(_combine_kernel, tm=tm, top_k=TOP_K)
    grid_spec = pltpu.PrefetchScalarGridSpec(
        num_scalar_prefetch=1, grid=(t // tm,),
        in_specs=[pl.BlockSpec((tm, d), lambda i, ds: (i, 0)), pl.BlockSpec((tm, LANES), lambda i, ds: (i, 0)),
                  pl.BlockSpec(memory_space=pl.ANY)],
        out_specs=pl.BlockSpec((tm, d), lambda i, ds: (i, 0)),
        scratch_shapes=[pltpu.VMEM((2, TOP_K * tm, d), F32), pltpu.SemaphoreType.DMA((2,))])
    return pl.pallas_call(kern, grid_spec=grid_spec, out_shape=jax.ShapeDtypeStruct((t, d), F32),
                          compiler_params=_params("arbitrary"), name="moe_combine")(dest, x, route, ys)


def moe_tables(route, counts, n_blocks, tm):
    bm = MOE_BLOCK
    t = route.shape[0]
    eid = route[:, 2:2 + TOP_K].astype(jnp.int32).T
    rank = route[:, 2 + TOP_K:2 + 2 * TOP_K].astype(jnp.int32).T
    cnt = counts[0, N_GROUPS:N_GROUPS + N_EXPERTS].astype(jnp.int32)
    padded = (cnt + bm - 1) // bm * bm
    pend = jnp.cumsum(padded)
    pstart = pend - padded
    seg = jnp.zeros_like(eid)
    for e in range(N_EXPERTS):
        seg = jnp.where(eid == e, pstart[e], seg)
    dest = (seg + rank).reshape(TOP_K, t // tm, tm).transpose(1, 0, 2).reshape(-1)
    dest = jnp.concatenate([dest, jnp.zeros((TOP_K * tm,), jnp.int32)])
    blk_start = jnp.arange(n_blocks, dtype=jnp.int32) * bm
    blk_exp = jnp.minimum(jnp.sum((pend[None, :] <= blk_start[:, None]).astype(jnp.int32), axis=1), N_EXPERTS - 1)
    n_used = (pend[-1] // bm).astype(jnp.int32).reshape(1)
    pad_lo = jnp.concatenate([pstart + cnt, pend[-1:]])
    pad_n = jnp.concatenate([padded - cnt, n_blocks * bm - pend[-1:]])
    return dest, blk_exp, n_used, pad_lo, pad_n


def moe_layer(x_mid, route, counts, ffn_norm, w_gate, w_up, w_down, layer):
    t = x_mid.shape[0]
    tm = _row_tile(t, MOE_BLOCK)
    n_blocks = -(-TOP_K * t // MOE_BLOCK) + N_EXPERTS
    dest, blk_exp, n_used, pad_lo, pad_n = moe_tables(route, counts, n_blocks, tm)
    xs = moe_dispatch(x_mid, dest, pad_lo, pad_n, n_blocks * MOE_BLOCK, tm)
    ys = moe_experts(xs, ffn_norm, blk_exp, n_used, w_gate, w_up, w_down, layer)
    return moe_combine(x_mid, route, ys, dest, tm)


DIFF_TQ = 512
DIFF_KS = 256
DIFF_CW = 256
DIFF_AHEAD = 4
DIFF_ONES = 16


def _diff_attn_kernel(q_ref, k_ref, v_ref, lam_ref, subln_ref, out_ref, vt_ref, qst_ref, m_ref, acc_ref, s_ref,
                      *, tq, ks, cw, ahead, lambda_init):
    hd = DIFF_HD
    dv = DIFF_DV
    seq = k_ref.shape[0]
    qi = pl.program_id(2)
    n_cols = 2 * tq

    @pl.when(qi == 0)
    def _():
        for j in range(seq // tq):
            vt_ref[0:dv, j * tq:(j + 1) * tq] = v_ref[j * tq:(j + 1) * tq, :].astype(F32).T.astype(BF16)
        vt_ref[dv:dv + DIFF_ONES, :] = jnp.ones((DIFF_ONES, seq), BF16)

    qt = q_ref[...].astype(F32).T
    row = lax.broadcasted_iota(jnp.int32, qt.shape, 0)
    qst_ref[:, 0:tq] = jnp.where(row < hd, qt, 0.0).astype(BF16)
    qst_ref[:, tq:n_cols] = jnp.where(row >= hd, qt, 0.0).astype(BF16)
    m_ref[...] = jnp.full_like(m_ref, -jnp.inf)
    acc_ref[...] = jnp.zeros_like(acc_ref)

    assert ahead == n_cols // cw and ks < tq, "the carried pieces are key piece 0 of every column chunk"

    def scores(off, kc, c):
        k = k_ref[pl.ds(off + kc * ks, ks), :]
        return jnp.dot(k, qst_ref[:, c * cw:(c + 1) * cw], preferred_element_type=F32)

    for c in range(ahead):
        s_ref[c] = scores(0, 0, c)

    def tile(ki, diagonal):
        off = pl.multiple_of(ki * tq, tq)
        items = []
        for kc in range(tq // ks):
            for c in range(n_cols // cw):
                q_lo = (c * cw) % tq
                if diagonal and q_lo + cw <= kc * ks:
                    continue
                items.append((kc, c, diagonal and (kc + 1) * ks - 1 > q_lo))

        pending = {}
        for n, (kc, c, masked) in enumerate(items):
            s = s_ref[n] if n < ahead else pending.pop(n)
            nxt = n + ahead
            if nxt < len(items):
                pending[nxt] = scores(off, *items[nxt][:2])
            elif not diagonal:
                s_ref[nxt - len(items)] = scores(off + tq, 0, nxt - len(items))
            cols = slice(c * cw, (c + 1) * cw)
            if masked:
                kpos = kc * ks + lax.broadcasted_iota(jnp.int32, s.shape, 0)
                qpos = (c * cw) % tq + lax.broadcasted_iota(jnp.int32, s.shape, 1)
                s = jnp.where(kpos <= qpos, s, -jnp.inf)
            m_prev = m_ref[:, cols]
            m_new = jnp.maximum(m_prev, jnp.max(s, axis=0, keepdims=True))
            alpha = jnp.exp2(m_prev - m_new)
            p = jnp.exp2(s - m_new)
            vt = vt_ref[:, pl.ds(off + kc * ks, ks)]
            acc_ref[:, cols] = alpha * acc_ref[:, cols] + jnp.dot(vt, p.astype(BF16), preferred_element_type=F32)
            m_ref[:, cols] = m_new

    lax.fori_loop(0, qi, lambda ki, c: (tile(ki, False), c)[1], 0)
    tile(qi, True)

    lv = lam_ref[...]
    lam = (jnp.exp(jnp.sum(lv[0:1] * lv[1:2], axis=1, keepdims=True))
           - jnp.exp(jnp.sum(lv[2:3] * lv[3:4], axis=1, keepdims=True)) + lambda_init)
    o = (acc_ref[0:dv, 0:tq] / acc_ref[dv:dv + 1, 0:tq]
         - lam * (acc_ref[0:dv, tq:n_cols] / acc_ref[dv:dv + 1, tq:n_cols]))
    o = o * lax.rsqrt(jnp.mean(o * o, axis=0, keepdims=True) + EPS) * subln_ref[...] * (1.0 - lambda_init)
    out_ref[...] = o.T.astype(out_ref.dtype)


def diff_attn(proj, lam_vecs, subln_col, lambda_init, batch, seq, tq):
    nh = DIFF_HEADS
    nq = seq // tq
    kern = functools.partial(_diff_attn_kernel, tq=tq, ks=min(DIFF_KS, tq), cw=min(DIFF_CW, tq), ahead=DIFF_AHEAD,
                             lambda_init=lambda_init)
    return pl.pallas_call(
        kern, grid=(batch, nh, nq),
        in_specs=[pl.BlockSpec((tq, LANES), lambda b, h, i: (b * nq + i, 2 * nh + h)),
                  pl.BlockSpec((seq, LANES), lambda b, h, i: (b, h)),
                  pl.BlockSpec((seq, LANES), lambda b, h, i: (b, nh + h)),
                  pl.BlockSpec((4, DIFF_HD), lambda b, h, i: (0, 0)),
                  pl.BlockSpec((DIFF_DV, 1), lambda b, h, i: (0, 0))],
        out_specs=pl.BlockSpec((tq, DIFF_DV), lambda b, h, i: (b * nq + i, h)),
        out_shape=jax.ShapeDtypeStruct((batch * seq, nh * DIFF_DV), BF16),
        scratch_shapes=[pltpu.VMEM((DIFF_DV + DIFF_ONES, seq), BF16), pltpu.VMEM((LANES, 2 * tq), BF16),
                        pltpu.VMEM((1, 2 * tq), F32), pltpu.VMEM((DIFF_DV + DIFF_ONES, 2 * tq), F32),
                        pltpu.VMEM((DIFF_AHEAD, min(DIFF_KS, tq), min(DIFF_CW, tq)), F32)],
        compiler_params=_params("parallel", "parallel", "arbitrary"), name="diff_attn")(
            proj, proj, proj, lam_vecs, subln_col)


def _row_tile(n, want):
    while n % want:
        want //= 2
    return want


def _router_weights(w_group, b_group, w_expert, b_expert):
    d = w_group.shape[0]
    pad = LANES - N_GROUPS - N_EXPERTS
    w = jnp.concatenate([w_group, w_expert, jnp.zeros((d, pad), F32)], axis=1).astype(BF16)
    b = jnp.concatenate([b_group, b_expert, jnp.zeros((pad,), F32)]).reshape(1, LANES)
    return w, b


def kernel(x, mem, positions, attn_norm, mem_norm, mem_w_kv, mem_qnorm, mem_knorm, a_w_in, a_gate_bias,
           a_head_norm, a_w_out, kv_norm, kv_w, kv_knorm, b_w_in, b_qnorm, b_lambda, b_subln, b_w_out, ffn_norm,
           moe_w_group, moe_b_group, moe_w_expert, moe_b_expert, moe_w_gate, moe_w_up, moe_w_down):
    batch, seq, d = x.shape
    slots = mem.shape[1]
    t = batch * seq
    tm = _row_tile(t, 512)
    a_qk = MLSTM_HEADS * MLSTM_DQK
    a_v = MLSTM_HEADS * MLSTM_DV
    mem_w = MEM_HEADS * MEM_HEAD_DIM
    diff_qk = 2 * DIFF_HEADS * DIFF_HD
    xt = x.reshape(t, d)
    w_gate, w_up, w_down = moe_w_gate.astype(BF16), moe_w_up.astype(BF16), moe_w_down.astype(BF16)

    memt = mem.reshape(batch * slots, d)
    w_mkv = jnp.concatenate([mem_w_kv[0], mem_w_kv[1]], axis=1).astype(BF16)
    (mkv,) = norm_proj(memt, mem_norm, w_mkv, tm=_row_tile(batch * slots, 512), tn=512,
                       norm_bounds=(2 * mem_w // 512,))

    g_lo = 2 * a_qk + 2 * a_v
    w_in = a_w_in[0]
    w_main = jnp.concatenate([w_in[:, :g_lo], w_in[:, g_lo + 2 * MLSTM_HEADS:]], axis=1).astype(BF16)
    w_gates = jnp.pad(w_in[:, g_lo:g_lo + 2 * MLSTM_HEADS], ((0, 0), (0, LANES - 2 * MLSTM_HEADS))).astype(BF16)
    proj0, gates = norm_proj(xt, attn_norm[0:1], w_main, tm=tm, tn=512, w_aux=w_gates)
    gate_bias = jnp.pad(a_gate_bias[0], (0, LANES - 2 * MLSTM_HEADS)).reshape(1, LANES)
    hm = mlstm(proj0, gates, gate_bias, a_head_norm[0], batch, seq)
    mo0 = mem_attn(proj0, g_lo // mem_w, mkv, 0, mem_qnorm[0:1], mem_knorm[0:1], batch, seq, slots, _row_tile(seq, 512))
    w_r0, b_r0 = _router_weights(moe_w_group[0], moe_b_group[0], moe_w_expert[0], moe_b_expert[0])
    w_out0 = a_w_out[0].astype(BF16)
    x_mid0, route0, counts0 = out_proj(hm, mo0, w_out0[:a_v], w_out0[a_v:], xt, ffn_norm[0:1], w_r0, b_r0, tm)
    x1 = moe_layer(x_mid0, route0, counts0, ffn_norm[0:1], w_gate, w_up, w_down, 0)

    half = DIFF_HD // 2
    lane = jnp.arange(LANES)
    inv_freq = ROPE_THETA ** (-(lane % half).astype(F32) / half)
    ang = positions.astype(F32).reshape(t, 1) * inv_freq[None, :]
    cos = jnp.cos(ang)
    sin = jnp.sin(ang) * jnp.where((lane % DIFF_HD) < half, -1.0, 1.0)[None, :]
    tn = 512
    w1 = jnp.concatenate([kv_w, b_w_in[0]], axis=1).astype(BF16)
    n_tiles = w1.shape[1] // tn
    k_tiles = diff_qk // tn
    q_lo = kv_w.shape[1] // tn
    reps = LANES // DIFF_HD
    ew = jnp.zeros((n_tiles, 1, LANES), F32)
    ew = ew.at[0:k_tiles].set(jnp.tile(kv_knorm, reps)[None, None, :])
    q_scale = DIFF_HD ** -0.5 * math.log2(math.e)
    ew = ew.at[q_lo:q_lo + k_tiles].set(jnp.tile(b_qnorm[0], reps)[None, None, :] * q_scale)
    norms1 = jnp.stack([kv_norm, attn_norm[1]])
    (proj1,) = norm_proj(x1, norms1, w1, tm=tm, tn=tn, norm_bounds=(q_lo,), rope=(cos, sin, ew),
                         rope_ranges=((0, k_tiles), (q_lo, q_lo + k_tiles)))

    lambda_init = 0.8 - 0.6 * math.exp(-0.3 * 1)
    oa = diff_attn(proj1, b_lambda[0], b_subln[0].reshape(DIFF_DV, 1), lambda_init, batch, seq,
                   _row_tile(seq, DIFF_TQ))
    mo1 = mem_attn(proj1, (kv_w.shape[1] + diff_qk) // mem_w, mkv, 1, mem_qnorm[1:2], mem_knorm[1:2], batch, seq,
                   slots, _row_tile(seq, 512))
    w_r1, b_r1 = _router_weights(moe_w_group[1], moe_b_group[1], moe_w_expert[1], moe_b_expert[1])
    w_out1 = b_w_out[0].astype(BF16)
    n_oa = DIFF_HEADS * DIFF_DV
    x_mid1, route1, counts1 = out_proj(oa, mo1, w_out1[:n_oa], w_out1[n_oa:], x1, ffn_norm[1:2], w_r1, b_r1, tm)
    out = moe_layer(x_mid1, route1, counts1, ffn_norm[1:2], w_gate, w_up, w_down, 1)
    return out.reshape(batch, seq, d)
```

```python
import functools
import math

import jax
import jax.numpy as jnp
from jax import lax
from jax.experimental import pallas as pl
from jax.experimental.pallas import tpu as pltpu

F32 = jnp.float32
BF16 = jnp.bfloat16
EPS = 1e-6
LANES = 128
VMEM_LIMIT_BYTES = 56 * 1024 * 1024

MLSTM_HEADS = 6
MLSTM_DQK = 128
MLSTM_DV = 256
MLSTM_CHUNK = 128
MEM_HEADS = 4
MEM_HEAD_DIM = 128
DIFF_HEADS = 12
DIFF_HD = 64
DIFF_DV = 128
ROPE_THETA = 10000.0
N_GROUPS = 4
EXPERTS_PER_GROUP = 8
N_EXPERTS = N_GROUPS * EXPERTS_PER_GROUP
TOP_K = 2
MOE_BLOCK = 256


def _params(*semantics):
    return pltpu.CompilerParams(dimension_semantics=semantics, vmem_limit_bytes=VMEM_LIMIT_BYTES)


def _norm_proj_kernel(*refs, norm_bounds, rope_ranges, has_aux, hd):
    it = iter(refs)
    x_ref = next(it)
    normw_ref = next(it)
    w_ref = next(it)
    if rope_ranges:
        cos_ref, sin_ref, ew_ref = next(it), next(it), next(it)
    if has_aux:
        waux_ref = next(it)
    out_ref = next(it)
    if has_aux:
        aux_ref = next(it)
    h_ref = next(it)

    j = pl.program_id(1)
    n_norm = len(norm_bounds) + 1

    @pl.when(j == 0)
    def _():
        x = x_ref[...]
        xn = x * lax.rsqrt(jnp.mean(x * x, axis=-1, keepdims=True) + EPS)
        for n in range(n_norm):
            h_ref[n] = (xn * normw_ref[n:n + 1, :]).astype(BF16)
        if has_aux:
            aux_ref[...] = jnp.dot(h_ref[0], waux_ref[...], preferred_element_type=F32)

    sel = jnp.int32(0)
    for b in norm_bounds:
        sel = sel + (j >= b).astype(jnp.int32)
    def plain():
        out_ref[...] = jnp.dot(h_ref[sel], w_ref[...], preferred_element_type=F32).astype(out_ref.dtype)

    if not rope_ranges:
        plain()
        return

    is_rope = jnp.bool_(False)
    for lo, hi in rope_ranges:
        is_rope = is_rope | ((j >= lo) & (j < hi))

    @pl.when(is_rope)
    def _():
        tm, tn = out_ref.shape
        ew = ew_ref[j]
        cs = cos_ref[...]
        sn = sin_ref[...]
        r = lax.broadcasted_iota(jnp.int32, (2 * LANES, LANES), 0)
        c = lax.broadcasted_iota(jnp.int32, (2 * LANES, LANES), 1)
        seg = ((r % LANES) // hd == c // hd).astype(BF16)
        lane = lax.broadcasted_iota(jnp.int32, (tm, LANES), 1)
        first_half = (lane % hd) < (hd // 2)
        acc = jnp.dot(h_ref[sel], w_ref[...], preferred_element_type=F32)
        for s in range(tn // LANES):
            t = acc[:, s * LANES:(s + 1) * LANES]
            tt = t * t
            hi_part = tt.astype(BF16)
            lo_part = (tt - hi_part.astype(F32)).astype(BF16)
            ss = jnp.dot(jnp.concatenate([hi_part, lo_part], axis=1), seg, preferred_element_type=F32)
            y = t * lax.rsqrt(ss * (1.0 / hd) + EPS) * ew
            rot = jnp.where(first_half, pltpu.roll(y, LANES - hd // 2, 1), pltpu.roll(y, hd // 2, 1))
            out_ref[:, s * LANES:(s + 1) * LANES] = (y * cs + rot * sn).astype(out_ref.dtype)

    pl.when(jnp.logical_not(is_rope))(plain)


def norm_proj(x, norm_w, w, *, tm, tn, norm_bounds=(), rope=None, rope_ranges=(), w_aux=None):
    n_rows, d = x.shape
    n = w.shape[1]
    n_norm = norm_w.shape[0]
    grid = (n_rows // tm, n // tn)
    in_specs = [pl.BlockSpec((tm, d), lambda i, j: (i, 0)), pl.BlockSpec((n_norm, d), lambda i, j: (0, 0)),
                pl.BlockSpec((d, tn), lambda i, j: (0, j))]
    args = [x, norm_w, w]
    if rope_ranges:
        cos, sin, ew = rope
        in_specs += [pl.BlockSpec((tm, LANES), lambda i, j: (i, 0)), pl.BlockSpec((tm, LANES), lambda i, j: (i, 0)),
                     pl.BlockSpec(ew.shape, lambda i, j: (0, 0, 0))]
        args += [cos, sin, ew]
    if w_aux is not None:
        in_specs.append(pl.BlockSpec((d, LANES), lambda i, j: (0, 0)))
        args.append(w_aux)
    out_shape = [jax.ShapeDtypeStruct((n_rows, n), BF16)]
    out_specs = [pl.BlockSpec((tm, tn), lambda i, j: (i, j))]
    if w_aux is not None:
        out_shape.append(jax.ShapeDtypeStruct((n_rows, LANES), F32))
        out_specs.append(pl.BlockSpec((tm, LANES), lambda i, j: (i, 0)))
    kern = functools.partial(_norm_proj_kernel, norm_bounds=tuple(norm_bounds), rope_ranges=tuple(rope_ranges),
                             has_aux=w_aux is not None, hd=DIFF_HD)
    return pl.pallas_call(
        kern, grid=grid, in_specs=in_specs, out_specs=out_specs, out_shape=out_shape,
        scratch_shapes=[pltpu.VMEM((n_norm, tm, d), BF16)],
        compiler_params=_params("parallel", "arbitrary"), name="norm_proj")(*args)


def _log_sigmoid(x):
    return jnp.minimum(x, 0.0) - jnp.log1p(jnp.exp(-jnp.abs(x)))


def _mlstm_kernel(q_ref, k_ref, v_ref, o_ref, g_ref, bias_ref, hnorm_ref, out_ref, c_ref, n_ref, m_ref):
    L, H, DQK, DV = MLSTM_CHUNK, MLSTM_HEADS, MLSTM_DQK, MLSTM_DV
    scale = DQK ** -0.5

    @pl.when(pl.program_id(1) == 0)
    def _():
        c_ref[...] = jnp.zeros_like(c_ref)
        n_ref[...] = jnp.zeros_like(n_ref)
        m_ref[...] = jnp.zeros_like(m_ref)

    g = g_ref[...] + bias_ref[...]
    ls = _log_sigmoid(g)
    g_t = g.T
    ls_t = ls.T
    row = lax.broadcasted_iota(jnp.int32, (L, L), 0)
    col = lax.broadcasted_iota(jnp.int32, (L, L), 1)
    causal = col <= row

    for h in range(H):
        q = q_ref[:, h * DQK:(h + 1) * DQK]
        k = k_ref[:, h * DQK:(h + 1) * DQK]
        v = v_ref[:, h * DV:(h + 1) * DV]
        i_col = g[:, h:h + 1]
        i_row = g_t[h:h + 1, :]
        lf_col = ls[:, H + h:H + h + 1]
        lf_row = ls_t[H + h:H + h + 1, :]
        b_col = jnp.sum(jnp.where(causal, lf_row, 0.0), axis=1, keepdims=True)
        b_row = jnp.sum(jnp.where(row <= col, lf_col, 0.0), axis=0, keepdims=True)
        m_prev = m_ref[h][:, 0:1]
        c_prev = c_ref[h]
        n_prev = n_ref[h]

        d_log = jnp.where(causal, b_col - b_row + i_row, -jnp.inf)
        inter = b_col + m_prev
        m_t = jnp.maximum(inter, jnp.max(d_log, axis=1, keepdims=True))
        w_intra = jnp.exp(d_log - m_t)
        w_inter = jnp.exp(inter - m_t) * scale
        qk = lax.dot_general(q, k, (((1,), (1,)), ((), ())), preferred_element_type=F32)
        s = qk * scale * w_intra
        num = jnp.dot(s.astype(BF16), v, preferred_element_type=F32)
        num = num + w_inter * jnp.dot(q, c_prev.astype(BF16), preferred_element_type=F32)
        den = jnp.sum(s, axis=1, keepdims=True) + w_inter * jnp.sum(q.astype(F32) * n_prev, axis=1, keepdims=True)
        hout = num / jnp.maximum(jnp.abs(den), jnp.exp(-m_t))

        b_last = b_row[:, L - 1:L]
        st_row = b_last - b_row + i_row
        m_new = jnp.maximum(b_last + m_prev, jnp.max(st_row, axis=1, keepdims=True))
        w_st_col = jnp.exp(b_last - b_col + i_col - m_new)
        decay = jnp.exp(b_last + m_prev - m_new)
        kw = k.astype(F32) * w_st_col
        c_ref[h] = decay * c_prev + lax.dot_general(kw.astype(BF16), v, (((0,), (0,)), ((), ())),
                                                    preferred_element_type=F32)
        n_ref[h] = decay * n_prev + jnp.sum(kw, axis=0, keepdims=True)
        m_ref[h] = jnp.broadcast_to(m_new, (1, LANES))

        hn = hout * lax.rsqrt(jnp.mean(hout * hout, axis=-1, keepdims=True) + EPS) * hnorm_ref[h:h + 1, :]
        og = o_ref[:, h * DV:(h + 1) * DV].astype(F32)
        out_ref[:, h * DV:(h + 1) * DV] = (hn * jax.nn.sigmoid(og)).astype(out_ref.dtype)


def mlstm(proj, gates, gate_bias, head_norm, batch, seq):
    L, H, DQK, DV = MLSTM_CHUNK, MLSTM_HEADS, MLSTM_DQK, MLSTM_DV
    nc = seq // L
    qk_w, v_w = H * DQK, H * DV
    row = lambda b, c: b * nc + c
    return pl.pallas_call(
        _mlstm_kernel, grid=(batch, nc),
        in_specs=[pl.BlockSpec((L, qk_w), lambda b, c: (row(b, c), 0)),
                  pl.BlockSpec((L, qk_w), lambda b, c: (row(b, c), 1)),
                  pl.BlockSpec((L, v_w), lambda b, c: (row(b, c), 1)),
                  pl.BlockSpec((L, v_w), lambda b, c: (row(b, c), 2)),
                  pl.BlockSpec((L, LANES), lambda b, c: (row(b, c), 0)),
                  pl.BlockSpec((1, LANES), lambda b, c: (0, 0)),
                  pl.BlockSpec((H, DV), lambda b, c: (0, 0))],
        out_specs=pl.BlockSpec((L, v_w), lambda b, c: (row(b, c), 0)),
        out_shape=jax.ShapeDtypeStruct((batch * seq, v_w), BF16),
        scratch_shapes=[pltpu.VMEM((H, DQK, DV), F32), pltpu.VMEM((H, 1, DQK), F32), pltpu.VMEM((H, 1, LANES), F32)],
        compiler_params=_params("parallel", "arbitrary"), name="mlstm")(
            proj, proj, proj, proj, gates, gate_bias, head_norm)


def _mem_attn_kernel(q_ref, k_ref, v_ref, qn_ref, kn_ref, out_ref):
    hd = MEM_HEAD_DIM
    scale = hd ** -0.5
    for h in range(MEM_HEADS):
        sl = slice(h * hd, (h + 1) * hd)
        q = q_ref[:, sl].astype(F32)
        q = q * lax.rsqrt(jnp.mean(q * q, axis=-1, keepdims=True) + EPS) * qn_ref[...]
        k = k_ref[:, sl].astype(F32)
        k = k * lax.rsqrt(jnp.mean(k * k, axis=-1, keepdims=True) + EPS) * kn_ref[...]
        s = lax.dot_general(q.astype(BF16), k.astype(BF16), (((1,), (1,)), ((), ())),
                            preferred_element_type=F32) * scale
        p = jnp.exp(s - jnp.max(s, axis=-1, keepdims=True))
        p = p / jnp.sum(p, axis=-1, keepdims=True)
        out_ref[:, sl] = jnp.dot(p.astype(BF16), v_ref[:, sl], preferred_element_type=F32).astype(out_ref.dtype)


def mem_attn(proj, q_col_block, mkv, layer, qnorm, knorm, batch, seq, slots, tm):
    width = MEM_HEADS * MEM_HEAD_DIM
    nt = seq // tm
    return pl.pallas_call(
        _mem_attn_kernel, grid=(batch, nt),
        in_specs=[pl.BlockSpec((tm, width), lambda b, i: (b * nt + i, q_col_block)),
                  pl.BlockSpec((slots, width), lambda b, i: (b, 2 * layer)),
                  pl.BlockSpec((slots, width), lambda b, i: (b, 2 * layer + 1)),
                  pl.BlockSpec((1, MEM_HEAD_DIM), lambda b, i: (0, 0)),
                  pl.BlockSpec((1, MEM_HEAD_DIM), lambda b, i: (0, 0))],
        out_specs=pl.BlockSpec((tm, width), lambda b, i: (b * nt + i, 0)),
        out_shape=jax.ShapeDtypeStruct((batch * seq, width), BF16),
        compiler_params=_params("parallel", "parallel"), name="mem_attn")(proj, mkv, mkv, qnorm, knorm)


def _out_proj_kernel(a_ref, m_ref, wa_ref, wm_ref, x_ref, fn_ref, wr_ref, br_ref, xo_ref, route_ref, cnt_ref,
                     base_ref):
    @pl.when(pl.program_id(0) == 0)
    def _():
        base_ref[...] = jnp.zeros_like(base_ref)

    y = jnp.dot(a_ref[...], wa_ref[...], preferred_element_type=F32)
    y = y + jnp.dot(m_ref[...], wm_ref[...], preferred_element_type=F32)
    x = x_ref[...] + y
    xo_ref[...] = x
    hn = (x * lax.rsqrt(jnp.mean(x * x, axis=-1, keepdims=True) + EPS) * fn_ref[...]).astype(BF16)
    lg = jnp.dot(hn, wr_ref[...], preferred_element_type=F32) + br_ref[...]
    lane = lax.broadcasted_iota(jnp.int32, lg.shape, 1).astype(F32)
    neg = -jnp.inf
    gl = jnp.where(lane < N_GROUPS, lg, neg)
    gmax = jnp.max(gl, axis=1, keepdims=True)
    g_w = 1.0 / jnp.sum(jnp.exp(gl - gmax), axis=1, keepdims=True)
    g_idx = jnp.min(jnp.where(gl == gmax, lane, float(LANES)), axis=1, keepdims=True)
    e_lo = N_GROUPS + EXPERTS_PER_GROUP * g_idx
    el = jnp.where((lane >= e_lo) & (lane < e_lo + EXPERTS_PER_GROUP), lg, neg)
    max1 = jnp.max(el, axis=1, keepdims=True)
    e1 = jnp.min(jnp.where(el == max1, lane, float(LANES)), axis=1, keepdims=True)
    el2 = jnp.where(lane == e1, neg, el)
    max2 = jnp.max(el2, axis=1, keepdims=True)
    e2 = jnp.min(jnp.where(el2 == max2, lane, float(LANES)), axis=1, keepdims=True)
    p2 = jnp.exp(max2 - max1)
    w1 = g_w / (1.0 + p2)
    w2 = g_w * p2 / (1.0 + p2)
    hot = ((lane == e1) | (lane == e2)).astype(BF16)
    tm = hot.shape[0]
    r = lax.broadcasted_iota(jnp.int32, (tm, tm), 0)
    c = lax.broadcasted_iota(jnp.int32, (tm, tm), 1)
    before = jnp.dot((c < r).astype(BF16), hot, preferred_element_type=F32) + base_ref[0:1, :]
    rank1 = jnp.sum(jnp.where(lane == e1, before, 0.0), axis=1, keepdims=True)
    rank2 = jnp.sum(jnp.where(lane == e2, before, 0.0), axis=1, keepdims=True)
    total = base_ref[0:1, :] + jnp.sum(hot.astype(F32), axis=0, keepdims=True)
    base_ref[...] = jnp.broadcast_to(total, base_ref.shape)
    cnt_ref[...] = jnp.broadcast_to(total, cnt_ref.shape)
    vals = (w1, w2, e1 - N_GROUPS, e2 - N_GROUPS, rank1, rank2)
    route = jnp.zeros_like(lg)
    for n, v in enumerate(vals):
        route = jnp.where(lane == n, v, route)
    route_ref[...] = route


def out_proj(a, m, w_a, w_m, x, ffn_norm, w_router, b_router, tm):
    t, d = x.shape
    ka, km = a.shape[1], m.shape[1]
    return pl.pallas_call(
        _out_proj_kernel, grid=(t // tm,),
        in_specs=[pl.BlockSpec((tm, ka), lambda i: (i, 0)), pl.BlockSpec((tm, km), lambda i: (i, 0)),
                  pl.BlockSpec((ka, d), lambda i: (0, 0)), pl.BlockSpec((km, d), lambda i: (0, 0)),
                  pl.BlockSpec((tm, d), lambda i: (i, 0)), pl.BlockSpec((1, d), lambda i: (0, 0)),
                  pl.BlockSpec((d, LANES), lambda i: (0, 0)), pl.BlockSpec((1, LANES), lambda i: (0, 0))],
        out_specs=[pl.BlockSpec((tm, d), lambda i: (i, 0)), pl.BlockSpec((tm, LANES), lambda i: (i, 0)),
                   pl.BlockSpec((8, LANES), lambda i: (0, 0))],
        out_shape=[jax.ShapeDtypeStruct((t, d), F32), jax.ShapeDtypeStruct((t, LANES), F32),
                   jax.ShapeDtypeStruct((8, LANES), F32)],
        scratch_shapes=[pltpu.VMEM((8, LANES), F32)],
        compiler_params=_params("arbitrary"), name="out_proj")(a, m, w_a, w_m, x, ffn_norm, w_router, b_router)


def _dispatch_kernel(dest_ref, pad_lo_ref, pad_n_ref, x_ref, xs_hbm, stage_ref, zero_ref, sem, zsem, *, tm, top_k):
    i = pl.program_id(0)
    n_rows = top_k * tm
    slot = i % 2

    def row_copy(step, slot, r):
        return pltpu.make_async_copy(stage_ref.at[slot, pl.ds(r % tm, 1)],
                                     xs_hbm.at[pl.ds(dest_ref[step * n_rows + r], 1)], sem.at[slot])

    def wait_step(slot):
        for _ in range(top_k):
            pltpu.make_async_copy(stage_ref.at[slot], xs_hbm.at[pl.ds(0, tm)], sem.at[slot]).wait()

    def zero_copy(e, r):
        return pltpu.make_async_copy(zero_ref.at[pl.ds(0, 1)], xs_hbm.at[pl.ds(pad_lo_ref[e] + r, 1)], zsem.at[0])

    def zero_tail(g):
        lo = pl.multiple_of(pad_lo_ref[N_EXPERTS] + g * zero_ref.shape[0], zero_ref.shape[0])
        return pltpu.make_async_copy(zero_ref, xs_hbm.at[pl.ds(lo, zero_ref.shape[0])], zsem.at[0])

    @pl.when(i == 0)
    def _():
        zero_ref[...] = jnp.zeros_like(zero_ref)
        tail_groups = pad_n_ref[N_EXPERTS] // zero_ref.shape[0]
        for e in range(N_EXPERTS):
            lax.fori_loop(0, pad_n_ref[e], lambda r, c, e=e: (zero_copy(e, r).start(), c)[1], 0)
        lax.fori_loop(0, tail_groups, lambda g, c: (zero_tail(g).start(), c)[1], 0)
        for e in range(N_EXPERTS):
            lax.fori_loop(0, pad_n_ref[e], lambda r, c, e=e: (zero_copy(e, r).wait(), c)[1], 0)
        lax.fori_loop(0, tail_groups, lambda g, c: (zero_tail(g).wait(), c)[1], 0)

    stage_ref[slot] = x_ref[...]
    for r in range(n_rows):
        row_copy(i, slot, r).start()

    @pl.when(i > 0)
    def _():
        wait_step(1 - slot)

    @pl.when(i == pl.num_programs(0) - 1)
    def _():
        wait_step(slot)


def moe_dispatch(x, dest, pad_lo, pad_n, n_sorted_rows, tm):
    t, d = x.shape
    kern = functools.partial(_dispatch_kernel, tm=tm, top_k=TOP_K)
    grid_spec = pltpu.PrefetchScalarGridSpec(
        num_scalar_prefetch=3, grid=(t // tm,),
        in_specs=[pl.BlockSpec((tm, d), lambda i, *_: (i, 0))], out_specs=pl.BlockSpec(memory_space=pl.ANY),
        scratch_shapes=[pltpu.VMEM((2, tm, d), F32), pltpu.VMEM((8, d), F32), pltpu.SemaphoreType.DMA((2,)),
                        pltpu.SemaphoreType.DMA((1,))])
    return pl.pallas_call(kern, grid_spec=grid_spec, out_shape=jax.ShapeDtypeStruct((n_sorted_rows, d), F32),
                          compiler_params=_params("arbitrary"), name="moe_dispatch")(dest, pad_lo, pad_n, x)


def _moe_kernel(blk_exp_ref, nused_ref, xs_ref, fn_ref, wg_ref, wu_ref, wd_ref, ys_ref):
    del blk_exp_ref

    @pl.when(pl.program_id(0) < nused_ref[0])
    def _():
        x = xs_ref[...]
        xn = (x * lax.rsqrt(jnp.mean(x * x, axis=-1, keepdims=True) + EPS) * fn_ref[...]).astype(BF16)
        gate = jnp.dot(xn, wg_ref[...], preferred_element_type=F32)
        up = jnp.dot(xn, wu_ref[...], preferred_element_type=F32)
        act = (gate * jax.nn.sigmoid(gate) * up).astype(BF16)
        ys_ref[...] = jnp.dot(act, wd_ref[...], preferred_element_type=F32)

    @pl.when(pl.program_id(0) >= nused_ref[0])
    def _():
        ys_ref[...] = jnp.zeros_like(ys_ref)


def moe_experts(xs, ffn_norm, blk_exp, n_used, w_gate, w_up, w_down, layer):
    bm = MOE_BLOCK
    p, d = xs.shape
    f = w_gate.shape[3]
    expert = lambda i, be, nu: (layer, be[i], 0, 0)
    grid_spec = pltpu.PrefetchScalarGridSpec(
        num_scalar_prefetch=2, grid=(p // bm,),
        in_specs=[pl.BlockSpec((bm, d), lambda i, be, nu: (jnp.minimum(i, nu[0] - 1), 0)),
                  pl.BlockSpec((1, d), lambda i, be, nu: (0, 0)),
                  pl.BlockSpec((None, None, d, f), expert),
                  pl.BlockSpec((None, None, d, f), expert),
                  pl.BlockSpec((None, None, f, d), expert)],
        out_specs=pl.BlockSpec((bm, d), lambda i, be, nu: (i, 0)))
    return pl.pallas_call(
        _moe_kernel, grid_spec=grid_spec, out_shape=jax.ShapeDtypeStruct((p, d), F32),
        compiler_params=_params("arbitrary"), name="moe_experts")(blk_exp, n_used, xs, ffn_norm, w_gate, w_up, w_down)


def _combine_kernel(dest_ref, x_ref, route_ref, ys_hbm, out_ref, ybuf, sem, *, tm, top_k):
    i = pl.program_id(0)
    n_rows = top_k * tm
    slot = i % 2

    def row_gather(step, slot, r):
        return pltpu.make_async_copy(ys_hbm.at[pl.ds(dest_ref[step * n_rows + r], 1)], ybuf.at[slot, pl.ds(r, 1)],
                                     sem.at[slot])

    def wait_gather(slot):
        pltpu.make_async_copy(ys_hbm.at[pl.ds(0, n_rows)], ybuf.at[slot], sem.at[slot]).wait()

    @pl.when(i == 0)
    def _():
        for r in range(n_rows):
            row_gather(0, 0, r).start()

    for r in range(n_rows):
        row_gather(i + 1, 1 - slot, r).start()
    wait_gather(slot)
    out = x_ref[...]
    for j in range(top_k):
        out = out + route_ref[:, j:j + 1] * ybuf[slot, j * tm:(j + 1) * tm, :]
    out_ref[...] = out

    @pl.when(i == pl.num_programs(0) - 1)
    def _():
        wait_gather(1 - slot)


def moe_combine(x, route, ys, dest, tm):
    t, d = x.shape
    kern = functools.partial(_combine_kernel, tm=tm, top_k=TOP_K)
    grid_spec = pltpu.PrefetchScalarGridSpec(
        num_scalar_prefetch=1, grid=(t // tm,),
        in_specs=[pl.BlockSpec((tm, d), lambda i, ds: (i, 0)), pl.BlockSpec((tm, LANES), lambda i, ds: (i, 0)),
                  pl.BlockSpec(memory_space=pl.ANY)],
        out_specs=pl.BlockSpec((tm, d), lambda i, ds: (i, 0)),
        scratch_shapes=[pltpu.VMEM((2, TOP_K * tm, d), F32), pltpu.SemaphoreType.DMA((2,))])
    return pl.pallas_call(kern, grid_spec=grid_spec, out_shape=jax.ShapeDtypeStruct((t, d), F32),
                          compiler_params=_params("arbitrary"), name="moe_combine")(dest, x, route, ys)


def moe_tables(route, counts, n_blocks, tm):
    bm = MOE_BLOCK
    t = route.shape[0]
    eid = route[:, 2:2 + TOP_K].astype(jnp.int32).T
    rank = route[:, 2 + TOP_K:2 + 2 * TOP_K].astype(jnp.int32).T
    cnt = counts[0, N_GROUPS:N_GROUPS + N_EXPERTS].astype(jnp.int32)
    padded = (cnt + bm - 1) // bm * bm
    pend = jnp.cumsum(padded)
    pstart = pend - padded
    seg = jnp.zeros_like(eid)
    for e in range(N_EXPERTS):
        seg = jnp.where(eid == e, pstart[e], seg)
    dest = (seg + rank).reshape(TOP_K, t // tm, tm).transpose(1, 0, 2).reshape(-1)
    dest = jnp.concatenate([dest, jnp.zeros((TOP_K * tm,), jnp.int32)])
    blk_start = jnp.arange(n_blocks, dtype=jnp.int32) * bm
    blk_exp = jnp.minimum(jnp.sum((pend[None, :] <= blk_start[:, None]).astype(jnp.int32), axis=1), N_EXPERTS - 1)
    n_used = (pend[-1] // bm).astype(jnp.int32).reshape(1)
    pad_lo = jnp.concatenate([pstart + cnt, pend[-1:]])
    pad_n = jnp.concatenate([padded - cnt, n_blocks * bm - pend[-1:]])
    return dest, blk_exp, n_used, pad_lo, pad_n


def moe_layer(x_mid, route, counts, ffn_norm, w_gate, w_up, w_down, layer):
    t = x_mid.shape[0]
    tm = _row_tile(t, MOE_BLOCK)
    n_blocks = -(-TOP_K * t // MOE_BLOCK) + N_EXPERTS
    dest, blk_exp, n_used, pad_lo, pad_n = moe_tables(route, counts, n_blocks, tm)
    xs = moe_dispatch(x_mid, dest, pad_lo, pad_n, n_blocks * MOE_BLOCK, tm)
    ys = moe_experts(xs, ffn_norm, blk_exp, n_used, w_gate, w_up, w_down, layer)
    return moe_combine(x_mid, route, ys, dest, tm)


DIFF_TQ = 512
DIFF_KS = 256
DIFF_CW = 256
DIFF_AHEAD = 4
DIFF_ONES = 16


def _diff_attn_kernel(q_ref, k_ref, v_ref, lam_ref, subln_ref, out_ref, vt_ref, qst_ref, m_ref, acc_ref, s_ref,
                      *, tq, ks, cw, ahead, lambda_init):
    hd = DIFF_HD
    dv = DIFF_DV
    seq = k_ref.shape[0]
    qi = pl.program_id(2)
    n_cols = 2 * tq

    @pl.when(qi == 0)
    def _():
        for j in range(seq // tq):
            vt_ref[0:dv, j * tq:(j + 1) * tq] = v_ref[j * tq:(j + 1) * tq, :].astype(F32).T.astype(BF16)
        vt_ref[dv:dv + DIFF_ONES, :] = jnp.ones((DIFF_ONES, seq), BF16)

    qt = q_ref[...].astype(F32).T
    row = lax.broadcasted_iota(jnp.int32, qt.shape, 0)
    qst_ref[:, 0:tq] = jnp.where(row < hd, qt, 0.0).astype(BF16)
    qst_ref[:, tq:n_cols] = jnp.where(row >= hd, qt, 0.0).astype(BF16)
    m_ref[...] = jnp.full_like(m_ref, -jnp.inf)
    acc_ref[...] = jnp.zeros_like(acc_ref)

    assert ahead == n_cols // cw and ks < tq, "the carried pieces are key piece 0 of every column chunk"

    def scores(off, kc, c):
        k = k_ref[pl.ds(off + kc * ks, ks), :]
        return jnp.dot(k, qst_ref[:, c * cw:(c + 1) * cw], preferred_element_type=F32)

    for c in range(ahead):
        s_ref[c] = scores(0, 0, c)

    def tile(ki, diagonal):
        off = pl.multiple_of(ki * tq, tq)
        items = []
        for kc in range(tq // ks):
            for c in range(n_cols // cw):
                q_lo = (c * cw) % tq
                if diagonal and q_lo + cw <= kc * ks:
                    continue
                items.append((kc, c, diagonal and (kc + 1) * ks - 1 > q_lo))

        pending = {}
        for n, (kc, c, masked) in enumerate(items):
            s = s_ref[n] if n < ahead else pending.pop(n)
            nxt = n + ahead
            if nxt < len(items):
                pending[nxt] = scores(off, *items[nxt][:2])
            elif not diagonal:
                s_ref[nxt - len(items)] = scores(off + tq, 0, nxt - len(items))
            cols = slice(c * cw, (c + 1) * cw)
            if masked:
                kpos = kc * ks + lax.broadcasted_iota(jnp.int32, s.shape, 0)
                qpos = (c * cw) % tq + lax.broadcasted_iota(jnp.int32, s.shape, 1)
                s = jnp.where(kpos <= qpos, s, -jnp.inf)
            m_prev = m_ref[:, cols]
            m_new = jnp.maximum(m_prev, jnp.max(s, axis=0, keepdims=True))
            alpha = jnp.exp2(m_prev - m_new)
            p = jnp.exp2(s - m_new)
            vt = vt_ref[:, pl.ds(off + kc * ks, ks)]
            acc_ref[:, cols] = alpha * acc_ref[:, cols] + jnp.dot(vt, p.astype(BF16), preferred_element_type=F32)
            m_ref[:, cols] = m_new

    lax.fori_loop(0, qi, lambda ki, c: (tile(ki, False), c)[1], 0)
    tile(qi, True)

    lv = lam_ref[...]
    lam = (jnp.exp(jnp.sum(lv[0:1] * lv[1:2], axis=1, keepdims=True))
           - jnp.exp(jnp.sum(lv[2:3] * lv[3:4], axis=1, keepdims=True)) + lambda_init)
    o = (acc_ref[0:dv, 0:tq] / acc_ref[dv:dv + 1, 0:tq]
         - lam * (acc_ref[0:dv, tq:n_cols] / acc_ref[dv:dv + 1, tq:n_cols]))
    o = o * lax.rsqrt(jnp.mean(o * o, axis=0, keepdims=True) + EPS) * subln_ref[...] * (1.0 - lambda_init)
    out_ref[...] = o.T.astype(out_ref.dtype)


def diff_attn(proj, lam_vecs, subln_col, lambda_init, batch, seq, tq):
    nh = DIFF_HEADS
    nq = seq // tq
    kern = functools.partial(_diff_attn_kernel, tq=tq, ks=min(DIFF_KS, tq), cw=min(DIFF_CW, tq), ahead=DIFF_AHEAD,
                             lambda_init=lambda_init)
    return pl.pallas_call(
        kern, grid=(batch, nh, nq),
        in_specs=[pl.BlockSpec((tq, LANES), lambda b, h, i: (b * nq + i, 2 * nh + h)),
                  pl.BlockSpec((seq, LANES), lambda b, h, i: (b, h)),
                  pl.BlockSpec((seq, LANES), lambda b, h, i: (b, nh + h)),
                  pl.BlockSpec((4, DIFF_HD), lambda b, h, i: (0, 0)),
                  pl.BlockSpec((DIFF_DV, 1), lambda b, h, i: (0, 0))],
        out_specs=pl.BlockSpec((tq, DIFF_DV), lambda b, h, i: (b * nq + i, h)),
        out_shape=jax.ShapeDtypeStruct((batch * seq, nh * DIFF_DV), BF16),
        scratch_shapes=[pltpu.VMEM((DIFF_DV + DIFF_ONES, seq), BF16), pltpu.VMEM((LANES, 2 * tq), BF16),
                        pltpu.VMEM((1, 2 * tq), F32), pltpu.VMEM((DIFF_DV + DIFF_ONES, 2 * tq), F32),
                        pltpu.VMEM((DIFF_AHEAD, min(DIFF_KS, tq), min(DIFF_CW, tq)), F32)],
        compiler_params=_params("parallel", "parallel", "arbitrary"), name="diff_attn")(
            proj, proj, proj, lam_vecs, subln_col)


def _row_tile(n, want):
    while n % want:
        want //= 2
    return want


def _router_weights(w_group, b_group, w_expert, b_expert):
    d = w_group.shape[0]
    pad = LANES - N_GROUPS - N_EXPERTS
    w = jnp.concatenate([w_group, w_expert, jnp.zeros((d, pad), F32)], axis=1).astype(BF16)
    b = jnp.concatenate([b_group, b_expert, jnp.zeros((pad,), F32)]).reshape(1, LANES)
    return w, b


def kernel(x, mem, positions, attn_norm, mem_norm, mem_w_kv, mem_qnorm, mem_knorm, a_w_in, a_gate_bias,
           a_head_norm, a_w_out, kv_norm, kv_w, kv_knorm, b_w_in, b_qnorm, b_lambda, b_subln, b_w_out, ffn_norm,
           moe_w_group, moe_b_group, moe_w_expert, moe_b_expert, moe_w_gate, moe_w_up, moe_w_down):
    batch, seq, d = x.shape
    slots = mem.shape[1]
    t = batch * seq
    tm = _row_tile(t, 512)
    tm_proj = _row_tile(t, 1024)
    a_qk = MLSTM_HEADS * MLSTM_DQK
    a_v = MLSTM_HEADS * MLSTM_DV
    mem_w = MEM_HEADS * MEM_HEAD_DIM
    diff_qk = 2 * DIFF_HEADS * DIFF_HD
    xt = x.reshape(t, d)
    w_gate, w_up, w_down = moe_w_gate.astype(BF16), moe_w_up.astype(BF16), moe_w_down.astype(BF16)

    memt = mem.reshape(batch * slots, d)
    w_mkv = jnp.concatenate([mem_w_kv[0], mem_w_kv[1]], axis=1).astype(BF16)
    (mkv,) = norm_proj(memt, mem_norm, w_mkv, tm=_row_tile(batch * slots, 512), tn=512,
                       norm_bounds=(2 * mem_w // 512,))

    g_lo = 2 * a_qk + 2 * a_v
    w_in = a_w_in[0]
    w_main = jnp.concatenate([w_in[:, :g_lo], w_in[:, g_lo + 2 * MLSTM_HEADS:]], axis=1).astype(BF16)
    w_gates = jnp.pad(w_in[:, g_lo:g_lo + 2 * MLSTM_HEADS], ((0, 0), (0, LANES - 2 * MLSTM_HEADS))).astype(BF16)
    proj0, gates = norm_proj(xt, attn_norm[0:1], w_main, tm=tm_proj, tn=512, w_aux=w_gates)
    gate_bias = jnp.pad(a_gate_bias[0], (0, LANES - 2 * MLSTM_HEADS)).reshape(1, LANES)
    hm = mlstm(proj0, gates, gate_bias, a_head_norm[0], batch, seq)
    mo0 = mem_attn(proj0, g_lo // mem_w, mkv, 0, mem_qnorm[0:1], mem_knorm[0:1], batch, seq, slots, _row_tile(seq, 512))
    w_r0, b_r0 = _router_weights(moe_w_group[0], moe_b_group[0], moe_w_expert[0], moe_b_expert[0])
    w_out0 = a_w_out[0].astype(BF16)
    x_mid0, route0, counts0 = out_proj(hm, mo0, w_out0[:a_v], w_out0[a_v:], xt, ffn_norm[0:1], w_r0, b_r0, tm)
    x1 = moe_layer(x_mid0, route0, counts0, ffn_norm[0:1], w_gate, w_up, w_down, 0)

    half = DIFF_HD // 2
    lane = jnp.arange(LANES)
    inv_freq = ROPE_THETA ** (-(lane % half).astype(F32) / half)
    ang = positions.astype(F32).reshape(t, 1) * inv_freq[None, :]
    cos = jnp.cos(ang)
    sin = jnp.sin(ang) * jnp.where((lane % DIFF_HD) < half, -1.0, 1.0)[None, :]
    tn = 512
    w1 = jnp.concatenate([kv_w, b_w_in[0]], axis=1).astype(BF16)
    n_tiles = w1.shape[1] // tn
    k_tiles = diff_qk // tn
    q_lo = kv_w.shape[1] // tn
    reps = LANES // DIFF_HD
    ew = jnp.zeros((n_tiles, 1, LANES), F32)
    ew = ew.at[0:k_tiles].set(jnp.tile(kv_knorm, reps)[None, None, :])
    q_scale = DIFF_HD ** -0.5 * math.log2(math.e)
    ew = ew.at[q_lo:q_lo + k_tiles].set(jnp.tile(b_qnorm[0], reps)[None, None, :] * q_scale)
    norms1 = jnp.stack([kv_norm, attn_norm[1]])
    (proj1,) = norm_proj(x1, norms1, w1, tm=tm_proj, tn=tn, norm_bounds=(q_lo,), rope=(cos, sin, ew),
                         rope_ranges=((0, k_tiles), (q_lo, q_lo + k_tiles)))

    lambda_init = 0.8 - 0.6 * math.exp(-0.3 * 1)
    oa = diff_attn(proj1, b_lambda[0], b_subln[0].reshape(DIFF_DV, 1), lambda_init, batch, seq,
                   _row_tile(seq, DIFF_TQ))
    mo1 = mem_attn(proj1, (kv_w.shape[1] + diff_qk) // mem_w, mkv, 1, mem_qnorm[1:2], mem_knorm[1:2], batch, seq,
                   slots, _row_tile(seq, 512))
    w_r1, b_r1 = _router_weights(moe_w_group[1], moe_b_group[1], moe_w_expert[1], moe_b_expert[1])
    w_out1 = b_w_out[0].astype(BF16)
    n_oa = DIFF_HEADS * DIFF_DV
    x_mid1, route1, counts1 = out_proj(oa, mo1, w_out1[:n_oa], w_out1[n_oa:], x1, ffn_norm[1:2], w_r1, b_r1, tm)
    out = moe_layer(x_mid1, route1, counts1, ffn_norm[1:2], w_gate, w_up, w_down, 1)
    return out.reshape(batch, seq, d)
```

```python
import functools
import math

import jax
import jax.numpy as jnp
from jax import lax
from jax.experimental import pallas as pl
from jax.experimental.pallas import tpu as pltpu

F32 = jnp.float32
BF16 = jnp.bfloat16
EPS = 1e-6
LANES = 128
VMEM_LIMIT_BYTES = 56 * 1024 * 1024

MLSTM_HEADS = 6
MLSTM_DQK = 128
MLSTM_DV = 256
MLSTM_CHUNK = 128
MEM_HEADS = 4
MEM_HEAD_DIM = 128
DIFF_HEADS = 12
DIFF_HD = 64
DIFF_DV = 128
ROPE_THETA = 10000.0
N_GROUPS = 4
EXPERTS_PER_GROUP = 8
N_EXPERTS = N_GROUPS * EXPERTS_PER_GROUP
TOP_K = 2
MOE_BLOCK = 256


def _params(*semantics):
    return pltpu.CompilerParams(dimension_semantics=semantics, vmem_limit_bytes=VMEM_LIMIT_BYTES)


def _norm_proj_kernel(*refs, norm_bounds, rope_ranges, has_aux, hd):
    it = iter(refs)
    x_ref = next(it)
    normw_ref = next(it)
    w_ref = next(it)
    if rope_ranges:
        cos_ref, sin_ref, ew_ref = next(it), next(it), next(it)
    if has_aux:
        waux_ref = next(it)
    out_ref = next(it)
    if has_aux:
        aux_ref = next(it)
    h_ref = next(it)

    j = pl.program_id(1)
    n_norm = len(norm_bounds) + 1

    @pl.when(j == 0)
    def _():
        x = x_ref[...]
        xn = x * lax.rsqrt(jnp.mean(x * x, axis=-1, keepdims=True) + EPS)
        for n in range(n_norm):
            h_ref[n] = (xn * normw_ref[n:n + 1, :]).astype(BF16)
        if has_aux:
            aux_ref[...] = jnp.dot(h_ref[0], waux_ref[...], preferred_element_type=F32)

    sel = jnp.int32(0)
    for b in norm_bounds:
        sel = sel + (j >= b).astype(jnp.int32)
    def plain():
        out_ref[...] = jnp.dot(h_ref[sel], w_ref[...], preferred_element_type=F32).astype(out_ref.dtype)

    if not rope_ranges:
        plain()
        return

    is_rope = jnp.bool_(False)
    for lo, hi in rope_ranges:
        is_rope = is_rope | ((j >= lo) & (j < hi))

    @pl.when(is_rope)
    def _():
        tm, tn = out_ref.shape
        ew = ew_ref[j]
        cs = cos_ref[...]
        sn = sin_ref[...]
        r = lax.broadcasted_iota(jnp.int32, (2 * LANES, LANES), 0)
        c = lax.broadcasted_iota(jnp.int32, (2 * LANES, LANES), 1)
        seg = ((r % LANES) // hd == c // hd).astype(BF16)
        lane = lax.broadcasted_iota(jnp.int32, (tm, LANES), 1)
        first_half = (lane % hd) < (hd // 2)
        acc = jnp.dot(h_ref[sel], w_ref[...], preferred_element_type=F32)
        for s in range(tn // LANES):
            t = acc[:, s * LANES:(s + 1) * LANES]
            tt = t * t
            hi_part = tt.astype(BF16)
            lo_part = (tt - hi_part.astype(F32)).astype(BF16)
            ss = jnp.dot(jnp.concatenate([hi_part, lo_part], axis=1), seg, preferred_element_type=F32)
            y = t * lax.rsqrt(ss * (1.0 / hd) + EPS) * ew
            rot = jnp.where(first_half, pltpu.roll(y, LANES - hd // 2, 1), pltpu.roll(y, hd // 2, 1))
            out_ref[:, s * LANES:(s + 1) * LANES] = (y * cs + rot * sn).astype(out_ref.dtype)

    pl.when(jnp.logical_not(is_rope))(plain)


def norm_proj(x, norm_w, w, *, tm, tn, norm_bounds=(), rope=None, rope_ranges=(), w_aux=None):
    n_rows, d = x.shape
    n = w.shape[1]
    n_norm = norm_w.shape[0]
    grid = (n_rows // tm, n // tn)
    in_specs = [pl.BlockSpec((tm, d), lambda i, j: (i, 0)), pl.BlockSpec((n_norm, d), lambda i, j: (0, 0)),
                pl.BlockSpec((d, tn), lambda i, j: (0, j))]
    args = [x, norm_w, w]
    if rope_ranges:
        cos, sin, ew = rope
        in_specs += [pl.BlockSpec((tm, LANES), lambda i, j: (i, 0)), pl.BlockSpec((tm, LANES), lambda i, j: (i, 0)),
                     pl.BlockSpec(ew.shape, lambda i, j: (0, 0, 0))]
        args += [cos, sin, ew]
    if w_aux is not None:
        in_specs.append(pl.BlockSpec((d, LANES), lambda i, j: (0, 0)))
        args.append(w_aux)
    out_shape = [jax.ShapeDtypeStruct((n_rows, n), BF16)]
    out_specs = [pl.BlockSpec((tm, tn), lambda i, j: (i, j))]
    if w_aux is not None:
        out_shape.append(jax.ShapeDtypeStruct((n_rows, LANES), F32))
        out_specs.append(pl.BlockSpec((tm, LANES), lambda i, j: (i, 0)))
    kern = functools.partial(_norm_proj_kernel, norm_bounds=tuple(norm_bounds), rope_ranges=tuple(rope_ranges),
                             has_aux=w_aux is not None, hd=DIFF_HD)
    return pl.pallas_call(
        kern, grid=grid, in_specs=in_specs, out_specs=out_specs, out_shape=out_shape,
        scratch_shapes=[pltpu.VMEM((n_norm, tm, d), BF16)],
        compiler_params=_params("parallel", "arbitrary"), name="norm_proj")(*args)


def _log_sigmoid(x):
    return jnp.minimum(x, 0.0) - jnp.log1p(jnp.exp(-jnp.abs(x)))


def _mlstm_kernel(q_ref, k_ref, v_ref, o_ref, g_ref, bias_ref, hnorm_ref, out_ref, c_ref, n_ref, m_ref):
    L, H, DQK, DV = MLSTM_CHUNK, MLSTM_HEADS, MLSTM_DQK, MLSTM_DV
    scale = DQK ** -0.5

    @pl.when(pl.program_id(1) == 0)
    def _():
        c_ref[...] = jnp.zeros_like(c_ref)
        n_ref[...] = jnp.zeros_like(n_ref)
        m_ref[...] = jnp.zeros_like(m_ref)

    g = g_ref[...] + bias_ref[...]
    ls = _log_sigmoid(g)
    g_t = g.T
    row = lax.broadcasted_iota(jnp.int32, (L, L), 0)
    col = lax.broadcasted_iota(jnp.int32, (L, L), 1)
    causal = col <= row
    ls_hi = ls.astype(BF16)
    ls_mid = (ls - ls_hi.astype(F32)).astype(BF16)
    ls_lo = (ls - ls_hi.astype(F32) - ls_mid.astype(F32)).astype(BF16)
    tri = causal.astype(BF16)
    cum = jnp.dot(jnp.concatenate([tri, tri, tri], axis=1), jnp.concatenate([ls_hi, ls_mid, ls_lo], axis=0),
                  preferred_element_type=F32)
    cum_t = cum.T

    qs = [q_ref[:, h * DQK:(h + 1) * DQK] for h in range(H)]
    ks = [k_ref[:, h * DQK:(h + 1) * DQK] for h in range(H)]
    vs = [v_ref[:, h * DV:(h + 1) * DV] for h in range(H)]
    qk = [lax.dot_general(qs[h], ks[h], (((1,), (1,)), ((), ())), preferred_element_type=F32) for h in range(H)]
    qc = [jnp.dot(qs[h], c_ref[h].astype(BF16), preferred_element_type=F32) for h in range(H)]

    gate = []
    for h in range(H):
        i_col = g[:, h:h + 1]
        i_row = g_t[h:h + 1, :]
        b_col = cum[:, H + h:H + h + 1]
        b_row = cum_t[H + h:H + h + 1, :]
        m_prev = m_ref[h][:, 0:1]
        d_log = jnp.where(causal, b_col - b_row + i_row, -jnp.inf)
        inter = b_col + m_prev
        m_t = jnp.maximum(inter, jnp.max(d_log, axis=1, keepdims=True))
        w_intra = jnp.exp(d_log - m_t)
        w_inter = jnp.exp(inter - m_t) * scale
        b_last = b_row[:, L - 1:L]
        st_row = b_last - b_row + i_row
        m_new = jnp.maximum(b_last + m_prev, jnp.max(st_row, axis=1, keepdims=True))
        w_st_col = jnp.exp(b_last - b_col + i_col - m_new)
        decay = jnp.exp(b_last + m_prev - m_new)
        gate.append((m_t, w_intra, w_inter, m_new, w_st_col, decay))

    s_all = [qk[h] * scale * gate[h][1] for h in range(H)]
    sv = [jnp.dot(s_all[h].astype(BF16), vs[h], preferred_element_type=F32) for h in range(H)]
    kw = [ks[h].astype(F32) * gate[h][4] for h in range(H)]
    kv = [lax.dot_general(kw[h].astype(BF16), vs[h], (((0,), (0,)), ((), ())), preferred_element_type=F32)
          for h in range(H)]

    for h in range(H):
        m_t, _, w_inter, m_new, _, decay = gate[h]
        n_prev = n_ref[h]
        num = sv[h] + w_inter * qc[h]
        den = (jnp.sum(s_all[h], axis=1, keepdims=True)
               + w_inter * jnp.sum(qs[h].astype(F32) * n_prev, axis=1, keepdims=True))
        hout = num * (1.0 / jnp.maximum(jnp.abs(den), jnp.exp(-m_t)))
        hn = hout * lax.rsqrt(jnp.mean(hout * hout, axis=-1, keepdims=True) + EPS) * hnorm_ref[h:h + 1, :]
        og = o_ref[:, h * DV:(h + 1) * DV].astype(F32)
        out_ref[:, h * DV:(h + 1) * DV] = (hn * jax.nn.sigmoid(og)).astype(out_ref.dtype)
        c_ref[h] = decay * c_ref[h] + kv[h]
        n_ref[h] = decay * n_prev + jnp.sum(kw[h], axis=0, keepdims=True)
        m_ref[h] = jnp.broadcast_to(m_new, (1, LANES))


def mlstm(proj, gates, gate_bias, head_norm, batch, seq):
    L, H, DQK, DV = MLSTM_CHUNK, MLSTM_HEADS, MLSTM_DQK, MLSTM_DV
    nc = seq // L
    qk_w, v_w = H * DQK, H * DV
    row = lambda b, c: b * nc + c
    return pl.pallas_call(
        _mlstm_kernel, grid=(batch, nc),
        in_specs=[pl.BlockSpec((L, qk_w), lambda b, c: (row(b, c), 0)),
                  pl.BlockSpec((L, qk_w), lambda b, c: (row(b, c), 1)),
                  pl.BlockSpec((L, v_w), lambda b, c: (row(b, c), 1)),
                  pl.BlockSpec((L, v_w), lambda b, c: (row(b, c), 2)),
                  pl.BlockSpec((L, LANES), lambda b, c: (row(b, c), 0)),
                  pl.BlockSpec((1, LANES), lambda b, c: (0, 0)),
                  pl.BlockSpec((H, DV), lambda b, c: (0, 0))],
        out_specs=pl.BlockSpec((L, v_w), lambda b, c: (row(b, c), 0)),
        out_shape=jax.ShapeDtypeStruct((batch * seq, v_w), BF16),
        scratch_shapes=[pltpu.VMEM((H, DQK, DV), F32), pltpu.VMEM((H, 1, DQK), F32), pltpu.VMEM((H, 1, LANES), F32)],
        compiler_params=_params("parallel", "arbitrary"), name="mlstm")(
            proj, proj, proj, proj, gates, gate_bias, head_norm)


def _mem_attn_kernel(q_ref, k_ref, v_ref, qn_ref, kn_ref, out_ref):
    hd = MEM_HEAD_DIM
    scale = hd ** -0.5
    for h in range(MEM_HEADS):
        sl = slice(h * hd, (h + 1) * hd)
        q = q_ref[:, sl].astype(F32)
        q = q * lax.rsqrt(jnp.mean(q * q, axis=-1, keepdims=True) + EPS) * qn_ref[...]
        k = k_ref[:, sl].astype(F32)
        k = k * lax.rsqrt(jnp.mean(k * k, axis=-1, keepdims=True) + EPS) * kn_ref[...]
        s = lax.dot_general(q.astype(BF16), k.astype(BF16), (((1,), (1,)), ((), ())),
                            preferred_element_type=F32) * scale
        p = jnp.exp(s - jnp.max(s, axis=-1, keepdims=True))
        p = p / jnp.sum(p, axis=-1, keepdims=True)
        out_ref[:, sl] = jnp.dot(p.astype(BF16), v_ref[:, sl], preferred_element_type=F32).astype(out_ref.dtype)


def mem_attn(proj, q_col_block, mkv, layer, qnorm, knorm, batch, seq, slots, tm):
    width = MEM_HEADS * MEM_HEAD_DIM
    nt = seq // tm
    return pl.pallas_call(
        _mem_attn_kernel, grid=(batch, nt),
        in_specs=[pl.BlockSpec((tm, width), lambda b, i: (b * nt + i, q_col_block)),
                  pl.BlockSpec((slots, width), lambda b, i: (b, 2 * layer)),
                  pl.BlockSpec((slots, width), lambda b, i: (b, 2 * layer + 1)),
                  pl.BlockSpec((1, MEM_HEAD_DIM), lambda b, i: (0, 0)),
                  pl.BlockSpec((1, MEM_HEAD_DIM), lambda b, i: (0, 0))],
        out_specs=pl.BlockSpec((tm, width), lambda b, i: (b * nt + i, 0)),
        out_shape=jax.ShapeDtypeStruct((batch * seq, width), BF16),
        compiler_params=_params("parallel", "parallel"), name="mem_attn")(proj, mkv, mkv, qnorm, knorm)


def _out_proj_kernel(a_ref, m_ref, wa_ref, wm_ref, x_ref, fn_ref, wr_ref, br_ref, xo_ref, route_ref, cnt_ref,
                     base_ref):
    @pl.when(pl.program_id(0) == 0)
    def _():
        base_ref[...] = jnp.zeros_like(base_ref)

    y = jnp.dot(a_ref[...], wa_ref[...], preferred_element_type=F32)
    y = y + jnp.dot(m_ref[...], wm_ref[...], preferred_element_type=F32)
    x = x_ref[...] + y
    xo_ref[...] = x
    hn = (x * lax.rsqrt(jnp.mean(x * x, axis=-1, keepdims=True) + EPS) * fn_ref[...]).astype(BF16)
    lg = jnp.dot(hn, wr_ref[...], preferred_element_type=F32) + br_ref[...]
    lane = lax.broadcasted_iota(jnp.int32, lg.shape, 1).astype(F32)
    neg = -jnp.inf
    gl = jnp.where(lane < N_GROUPS, lg, neg)
    gmax = jnp.max(gl, axis=1, keepdims=True)
    g_w = 1.0 / jnp.sum(jnp.exp(gl - gmax), axis=1, keepdims=True)
    g_idx = jnp.min(jnp.where(gl == gmax, lane, float(LANES)), axis=1, keepdims=True)
    e_lo = N_GROUPS + EXPERTS_PER_GROUP * g_idx
    el = jnp.where((lane >= e_lo) & (lane < e_lo + EXPERTS_PER_GROUP), lg, neg)
    max1 = jnp.max(el, axis=1, keepdims=True)
    e1 = jnp.min(jnp.where(el == max1, lane, float(LANES)), axis=1, keepdims=True)
    el2 = jnp.where(lane == e1, neg, el)
    max2 = jnp.max(el2, axis=1, keepdims=True)
    e2 = jnp.min(jnp.where(el2 == max2, lane, float(LANES)), axis=1, keepdims=True)
    p2 = jnp.exp(max2 - max1)
    w1 = g_w / (1.0 + p2)
    w2 = g_w * p2 / (1.0 + p2)
    hot = ((lane == e1) | (lane == e2)).astype(BF16)
    tm = hot.shape[0]
    r = lax.broadcasted_iota(jnp.int32, (tm, tm), 0)
    c = lax.broadcasted_iota(jnp.int32, (tm, tm), 1)
    before = jnp.dot((c < r).astype(BF16), hot, preferred_element_type=F32) + base_ref[0:1, :]
    rank1 = jnp.sum(jnp.where(lane == e1, before, 0.0), axis=1, keepdims=True)
    rank2 = jnp.sum(jnp.where(lane == e2, before, 0.0), axis=1, keepdims=True)
    total = base_ref[0:1, :] + jnp.sum(hot.astype(F32), axis=0, keepdims=True)
    base_ref[...] = jnp.broadcast_to(total, base_ref.shape)
    cnt_ref[...] = jnp.broadcast_to(total, cnt_ref.shape)
    vals = (w1, w2, e1 - N_GROUPS, e2 - N_GROUPS, rank1, rank2)
    route = jnp.zeros_like(lg)
    for n, v in enumerate(vals):
        route = jnp.where(lane == n, v, route)
    route_ref[...] = route


def out_proj(a, m, w_a, w_m, x, ffn_norm, w_router, b_router, tm):
    t, d = x.shape
    ka, km = a.shape[1], m.shape[1]
    return pl.pallas_call(
        _out_proj_kernel, grid=(t // tm,),
        in_specs=[pl.BlockSpec((tm, ka), lambda i: (i, 0)), pl.BlockSpec((tm, km), lambda i: (i, 0)),
                  pl.BlockSpec((ka, d), lambda i: (0, 0)), pl.BlockSpec((km, d), lambda i: (0, 0)),
                  pl.BlockSpec((tm, d), lambda i: (i, 0)), pl.BlockSpec((1, d), lambda i: (0, 0)),
                  pl.BlockSpec((d, LANES), lambda i: (0, 0)), pl.BlockSpec((1, LANES), lambda i: (0, 0))],
        out_specs=[pl.BlockSpec((tm, d), lambda i: (i, 0)), pl.BlockSpec((tm, LANES), lambda i: (i, 0)),
                   pl.BlockSpec((8, LANES), lambda i: (0, 0))],
        out_shape=[jax.ShapeDtypeStruct((t, d), F32), jax.ShapeDtypeStruct((t, LANES), F32),
                   jax.ShapeDtypeStruct((8, LANES), F32)],
        scratch_shapes=[pltpu.VMEM((8, LANES), F32)],
        compiler_params=_params("arbitrary"), name="out_proj")(a, m, w_a, w_m, x, ffn_norm, w_router, b_router)


def _dispatch_kernel(dest_ref, pad_lo_ref, pad_n_ref, x_ref, xs_hbm, stage_ref, zero_ref, sem, zsem, *, tm, top_k):
    i = pl.program_id(0)
    n_rows = top_k * tm
    slot = i % 2

    def row_copy(step, slot, r):
        return pltpu.make_async_copy(stage_ref.at[slot, pl.ds(r % tm, 1)],
                                     xs_hbm.at[pl.ds(dest_ref[step * n_rows + r], 1)], sem.at[slot])

    def wait_step(slot):
        for _ in range(top_k):
            pltpu.make_async_copy(stage_ref.at[slot], xs_hbm.at[pl.ds(0, tm)], sem.at[slot]).wait()

    def zero_copy(e, r):
        return pltpu.make_async_copy(zero_ref.at[pl.ds(0, 1)], xs_hbm.at[pl.ds(pad_lo_ref[e] + r, 1)], zsem.at[0])

    def zero_tail(g):
        lo = pl.multiple_of(pad_lo_ref[N_EXPERTS] + g * zero_ref.shape[0], zero_ref.shape[0])
        return pltpu.make_async_copy(zero_ref, xs_hbm.at[pl.ds(lo, zero_ref.shape[0])], zsem.at[0])

    @pl.when(i == 0)
    def _():
        zero_ref[...] = jnp.zeros_like(zero_ref)
        tail_groups = pad_n_ref[N_EXPERTS] // zero_ref.shape[0]
        for e in range(N_EXPERTS):
            lax.fori_loop(0, pad_n_ref[e], lambda r, c, e=e: (zero_copy(e, r).start(), c)[1], 0)
        lax.fori_loop(0, tail_groups, lambda g, c: (zero_tail(g).start(), c)[1], 0)
        for e in range(N_EXPERTS):
            lax.fori_loop(0, pad_n_ref[e], lambda r, c, e=e: (zero_copy(e, r).wait(), c)[1], 0)
        lax.fori_loop(0, tail_groups, lambda g, c: (zero_tail(g).wait(), c)[1], 0)

    stage_ref[slot] = x_ref[...]
    for r in range(n_rows):
        row_copy(i, slot, r).start()

    @pl.when(i > 0)
    def _():
        wait_step(1 - slot)

    @pl.when(i == pl.num_programs(0) - 1)
    def _():
        wait_step(slot)


def moe_dispatch(x, dest, pad_lo, pad_n, n_sorted_rows, tm):
    t, d = x.shape
    kern = functools.partial(_dispatch_kernel, tm=tm, top_k=TOP_K)
    grid_spec = pltpu.PrefetchScalarGridSpec(
        num_scalar_prefetch=3, grid=(t // tm,),
        in_specs=[pl.BlockSpec((tm, d), lambda i, *_: (i, 0))], out_specs=pl.BlockSpec(memory_space=pl.ANY),
        scratch_shapes=[pltpu.VMEM((2, tm, d), F32), pltpu.VMEM((8, d), F32), pltpu.SemaphoreType.DMA((2,)),
                        pltpu.SemaphoreType.DMA((1,))])
    return pl.pallas_call(kern, grid_spec=grid_spec, out_shape=jax.ShapeDtypeStruct((n_sorted_rows, d), F32),
                          compiler_params=_params("arbitrary"), name="moe_dispatch")(dest, pad_lo, pad_n, x)


def _moe_kernel(blk_exp_ref, nused_ref, xs_ref, fn_ref, wg_ref, wu_ref, wd_ref, ys_ref):
    del blk_exp_ref

    @pl.when(pl.program_id(0) < nused_ref[0])
    def _():
        x = xs_ref[...]
        xn = (x * lax.rsqrt(jnp.mean(x * x, axis=-1, keepdims=True) + EPS) * fn_ref[...]).astype(BF16)
        gate = jnp.dot(xn, wg_ref[...], preferred_element_type=F32)
        up = jnp.dot(xn, wu_ref[...], preferred_element_type=F32)
        act = (gate * jax.nn.sigmoid(gate) * up).astype(BF16)
        ys_ref[...] = jnp.dot(act, wd_ref[...], preferred_element_type=F32)

    @pl.when(pl.program_id(0) >= nused_ref[0])
    def _():
        ys_ref[...] = jnp.zeros_like(ys_ref)


def moe_experts(xs, ffn_norm, blk_exp, n_used, w_gate, w_up, w_down, layer):
    bm = MOE_BLOCK
    p, d = xs.shape
    f = w_gate.shape[3]
    expert = lambda i, be, nu: (layer, be[i], 0, 0)
    grid_spec = pltpu.PrefetchScalarGridSpec(
        num_scalar_prefetch=2, grid=(p // bm,),
        in_specs=[pl.BlockSpec((bm, d), lambda i, be, nu: (jnp.minimum(i, nu[0] - 1), 0)),
                  pl.BlockSpec((1, d), lambda i, be, nu: (0, 0)),
                  pl.BlockSpec((None, None, d, f), expert),
                  pl.BlockSpec((None, None, d, f), expert),
                  pl.BlockSpec((None, None, f, d), expert)],
        out_specs=pl.BlockSpec((bm, d), lambda i, be, nu: (i, 0)))
    return pl.pallas_call(
        _moe_kernel, grid_spec=grid_spec, out_shape=jax.ShapeDtypeStruct((p, d), F32),
        compiler_params=_params("arbitrary"), name="moe_experts")(blk_exp, n_used, xs, ffn_norm, w_gate, w_up, w_down)


def _combine_kernel(dest_ref, x_ref, route_ref, ys_hbm, out_ref, ybuf, sem, *, tm, top_k):
    i = pl.program_id(0)
    n_rows = top_k * tm
    slot = i % 2

    def row_gather(step, slot, r):
        return pltpu.make_async_copy(ys_hbm.at[pl.ds(dest_ref[step * n_rows + r], 1)], ybuf.at[slot, pl.ds(r, 1)],
                                     sem.at[slot])

    def wait_gather(slot):
        pltpu.make_async_copy(ys_hbm.at[pl.ds(0, n_rows)], ybuf.at[slot], sem.at[slot]).wait()

    @pl.when(i == 0)
    def _():
        for r in range(n_rows):
            row_gather(0, 0, r).start()

    for r in range(n_rows):
        row_gather(i + 1, 1 - slot, r).start()
    wait_gather(slot)
    out = x_ref[...]
    for j in range(top_k):
        out = out + route_ref[:, j:j + 1] * ybuf[slot, j * tm:(j + 1) * tm, :]
    out_ref[...] = out

    @pl.when(i == pl.num_programs(0) - 1)
    def _():
        wait_gather(1 - slot)


def moe_combine(x, route, ys, dest, tm):
    t, d = x.shape
    kern = functools.partial(_combine_kernel, tm=tm, top_k=TOP_K)
    grid_spec = pltpu.PrefetchScalarGridSpec(
        num_scalar_prefetch=1, grid=(t // tm,),
        in_specs=[pl.BlockSpec((tm, d), lambda i, ds: (i, 0)), pl.BlockSpec((tm, LANES), lambda i, ds: (i, 0)),
                  pl.BlockSpec(memory_space=pl.ANY)],
        out_specs=pl.BlockSpec((tm, d), lambda i, ds: (i, 0)),
        scratch_shapes=[pltpu.VMEM((2, TOP_K * tm, d), F32), pltpu.SemaphoreType.DMA((2,))])
    return pl.pallas_call(kern, grid_spec=grid_spec, out_shape=jax.ShapeDtypeStruct((t, d), F32),
                          compiler_params=_params("arbitrary"), name="moe_combine")(dest, x, route, ys)


def moe_tables(route, counts, n_blocks, tm):
    bm = MOE_BLOCK
    t = route.shape[0]
    eid = route[:, 2:2 + TOP_K].astype(jnp.int32).T
    rank = route[:, 2 + TOP_K:2 + 2 * TOP_K].astype(jnp.int32).T
    cnt = counts[0, N_GROUPS:N_GROUPS + N_EXPERTS].astype(jnp.int32)
    padded = (cnt + bm - 1) // bm * bm
    pend = jnp.cumsum(padded)
    pstart = pend - padded
    seg = jnp.zeros_like(eid)
    for e in range(N_EXPERTS):
        seg = jnp.where(eid == e, pstart[e], seg)
    dest = (seg + rank).reshape(TOP_K, t // tm, tm).transpose(1, 0, 2).reshape(-1)
    dest = jnp.concatenate([dest, jnp.zeros((TOP_K * tm,), jnp.int32)])
    blk_start = jnp.arange(n_blocks, dtype=jnp.int32) * bm
    blk_exp = jnp.minimum(jnp.sum((pend[None, :] <= blk_start[:, None]).astype(jnp.int32), axis=1), N_EXPERTS - 1)
    n_used = (pend[-1] // bm).astype(jnp.int32).reshape(1)
    pad_lo = jnp.concatenate([pstart + cnt, pend[-1:]])
    pad_n = jnp.concatenate([padded - cnt, n_blocks * bm - pend[-1:]])
    return dest, blk_exp, n_used, pad_lo, pad_n


def moe_layer(x_mid, route, counts, ffn_norm, w_gate, w_up, w_down, layer):
    t = x_mid.shape[0]
    tm = _row_tile(t, MOE_BLOCK)
    n_blocks = -(-TOP_K * t // MOE_BLOCK) + N_EXPERTS
    dest, blk_exp, n_used, pad_lo, pad_n = moe_tables(route, counts, n_blocks, tm)
    xs = moe_dispatch(x_mid, dest, pad_lo, pad_n, n_blocks * MOE_BLOCK, tm)
    ys = moe_experts(xs, ffn_norm, blk_exp, n_used, w_gate, w_up, w_down, layer)
    return moe_combine(x_mid, route, ys, dest, tm)


DIFF_TQ = 1024
DIFF_KS = 256
DIFF_CW = 256
DIFF_AHEAD = 8
DIFF_ONES = 16


def _diff_attn_kernel(q_ref, k_ref, v_ref, lam_ref, subln_ref, out_ref, vt_ref, qst_ref, m_ref, acc_ref, s_ref,
                      *, tq, ks, cw, ahead, lambda_init):
    hd = DIFF_HD
    dv = DIFF_DV
    seq = k_ref.shape[0]
    qi = pl.program_id(2)
    n_cols = 2 * tq

    @pl.when(qi == 0)
    def _():
        for j in range(seq // tq):
            vt_ref[0:dv, j * tq:(j + 1) * tq] = v_ref[j * tq:(j + 1) * tq, :].astype(F32).T.astype(BF16)
        vt_ref[dv:dv + DIFF_ONES, :] = jnp.ones((DIFF_ONES, seq), BF16)

    qt = q_ref[...].astype(F32).T
    row = lax.broadcasted_iota(jnp.int32, qt.shape, 0)
    qst_ref[:, 0:tq] = jnp.where(row < hd, qt, 0.0).astype(BF16)
    qst_ref[:, tq:n_cols] = jnp.where(row >= hd, qt, 0.0).astype(BF16)
    m_ref[...] = jnp.full_like(m_ref, -jnp.inf)
    acc_ref[...] = jnp.zeros_like(acc_ref)

    assert ahead == n_cols // cw and ks < tq, "the carried pieces are key piece 0 of every column chunk"

    def scores(off, kc, c):
        k = k_ref[pl.ds(off + kc * ks, ks), :]
        return jnp.dot(k, qst_ref[:, c * cw:(c + 1) * cw], preferred_element_type=F32)

    for c in range(ahead):
        s_ref[c] = scores(0, 0, c)

    def tiles(first, n_tiles, last_diagonal):
        off0 = pl.multiple_of(first * tq, tq)
        items = []
        for tt in range(n_tiles):
            diagonal = last_diagonal and tt == n_tiles - 1
            for kc in range(tq // ks):
                for c in range(n_cols // cw):
                    q_lo = (c * cw) % tq
                    if diagonal and q_lo + cw <= kc * ks:
                        continue
                    items.append((tt, kc, c, diagonal and (kc + 1) * ks - 1 > q_lo))

        pending = {}
        for n, (tt, kc, c, masked) in enumerate(items):
            off = off0 + tt * tq
            s = s_ref[n] if n < ahead else pending.pop(n)
            nxt = n + ahead
            if nxt < len(items):
                pending[nxt] = scores(off0 + items[nxt][0] * tq, items[nxt][1], items[nxt][2])
            elif not last_diagonal:
                s_ref[nxt - len(items)] = scores(off0 + n_tiles * tq, 0, nxt - len(items))
            cols = slice(c * cw, (c + 1) * cw)
            if masked:
                kpos = kc * ks + lax.broadcasted_iota(jnp.int32, s.shape, 0)
                qpos = (c * cw) % tq + lax.broadcasted_iota(jnp.int32, s.shape, 1)
                s = jnp.where(kpos <= qpos, s, -jnp.inf)
            m_prev = m_ref[:, cols]
            m_new = jnp.maximum(m_prev, jnp.max(s, axis=0, keepdims=True))
            alpha = jnp.exp2(m_prev - m_new)
            p = jnp.exp2(s - m_new)
            vt = vt_ref[:, pl.ds(off + kc * ks, ks)]
            acc_ref[:, cols] = alpha * acc_ref[:, cols] + jnp.dot(vt, p.astype(BF16), preferred_element_type=F32)
            m_ref[:, cols] = m_new

    lax.fori_loop(0, qi, lambda ki, c: (tiles(ki, 1, False), c)[1], 0)
    tiles(qi, 1, True)

    lv = lam_ref[...]
    lam = (jnp.exp(jnp.sum(lv[0:1] * lv[1:2], axis=1, keepdims=True))
           - jnp.exp(jnp.sum(lv[2:3] * lv[3:4], axis=1, keepdims=True)) + lambda_init)
    o = (acc_ref[0:dv, 0:tq] / acc_ref[dv:dv + 1, 0:tq]
         - lam * (acc_ref[0:dv, tq:n_cols] / acc_ref[dv:dv + 1, tq:n_cols]))
    o = o * lax.rsqrt(jnp.mean(o * o, axis=0, keepdims=True) + EPS) * subln_ref[...] * (1.0 - lambda_init)
    out_ref[...] = o.T.astype(out_ref.dtype)


def diff_attn(proj, lam_vecs, subln_col, lambda_init, batch, seq, tq):
    nh = DIFF_HEADS
    nq = seq // tq
    kern = functools.partial(_diff_attn_kernel, tq=tq, ks=min(DIFF_KS, tq), cw=min(DIFF_CW, tq), ahead=DIFF_AHEAD,
                             lambda_init=lambda_init)
    return pl.pallas_call(
        kern, grid=(batch, nh, nq),
        in_specs=[pl.BlockSpec((tq, LANES), lambda b, h, i: (b * nq + i, 2 * nh + h)),
                  pl.BlockSpec((seq, LANES), lambda b, h, i: (b, h)),
                  pl.BlockSpec((seq, LANES), lambda b, h, i: (b, nh + h)),
                  pl.BlockSpec((4, DIFF_HD), lambda b, h, i: (0, 0)),
                  pl.BlockSpec((DIFF_DV, 1), lambda b, h, i: (0, 0))],
        out_specs=pl.BlockSpec((tq, DIFF_DV), lambda b, h, i: (b * nq + i, h)),
        out_shape=jax.ShapeDtypeStruct((batch * seq, nh * DIFF_DV), BF16),
        scratch_shapes=[pltpu.VMEM((DIFF_DV + DIFF_ONES, seq), BF16), pltpu.VMEM((LANES, 2 * tq), BF16),
                        pltpu.VMEM((1, 2 * tq), F32), pltpu.VMEM((DIFF_DV + DIFF_ONES, 2 * tq), F32),
                        pltpu.VMEM((DIFF_AHEAD, min(DIFF_KS, tq), min(DIFF_CW, tq)), F32)],
        compiler_params=_params("parallel", "parallel", "arbitrary"), name="diff_attn")(
            proj, proj, proj, lam_vecs, subln_col)


def _row_tile(n, want):
    while n % want:
        want //= 2
    return want


def _router_weights(w_group, b_group, w_expert, b_expert):
    d = w_group.shape[0]
    pad = LANES - N_GROUPS - N_EXPERTS
    w = jnp.concatenate([w_group, w_expert, jnp.zeros((d, pad), F32)], axis=1).astype(BF16)
    b = jnp.concatenate([b_group, b_expert, jnp.zeros((pad,), F32)]).reshape(1, LANES)
    return w, b


def kernel(x, mem, positions, attn_norm, mem_norm, mem_w_kv, mem_qnorm, mem_knorm, a_w_in, a_gate_bias,
           a_head_norm, a_w_out, kv_norm, kv_w, kv_knorm, b_w_in, b_qnorm, b_lambda, b_subln, b_w_out, ffn_norm,
           moe_w_group, moe_b_group, moe_w_expert, moe_b_expert, moe_w_gate, moe_w_up, moe_w_down):
    batch, seq, d = x.shape
    slots = mem.shape[1]
    t = batch * seq
    tm = _row_tile(t, 512)
    tm_proj = _row_tile(t, 1024)
    a_qk = MLSTM_HEADS * MLSTM_DQK
    a_v = MLSTM_HEADS * MLSTM_DV
    mem_w = MEM_HEADS * MEM_HEAD_DIM
    diff_qk = 2 * DIFF_HEADS * DIFF_HD
    xt = x.reshape(t, d)
    w_gate, w_up, w_down = moe_w_gate.astype(BF16), moe_w_up.astype(BF16), moe_w_down.astype(BF16)

    memt = mem.reshape(batch * slots, d)
    w_mkv = jnp.concatenate([mem_w_kv[0], mem_w_kv[1]], axis=1).astype(BF16)
    (mkv,) = norm_proj(memt, mem_norm, w_mkv, tm=_row_tile(batch * slots, 512), tn=512,
                       norm_bounds=(2 * mem_w // 512,))

    g_lo = 2 * a_qk + 2 * a_v
    w_in = a_w_in[0]
    w_main = jnp.concatenate([w_in[:, :g_lo], w_in[:, g_lo + 2 * MLSTM_HEADS:]], axis=1).astype(BF16)
    w_gates = jnp.pad(w_in[:, g_lo:g_lo + 2 * MLSTM_HEADS], ((0, 0), (0, LANES - 2 * MLSTM_HEADS))).astype(BF16)
    proj0, gates = norm_proj(xt, attn_norm[0:1], w_main, tm=tm_proj, tn=512, w_aux=w_gates)
    gate_bias = jnp.pad(a_gate_bias[0], (0, LANES - 2 * MLSTM_HEADS)).reshape(1, LANES)
    hm = mlstm(proj0, gates, gate_bias, a_head_norm[0], batch, seq)
    mo0 = mem_attn(proj0, g_lo // mem_w, mkv, 0, mem_qnorm[0:1], mem_knorm[0:1], batch, seq, slots, _row_tile(seq, 512))
    w_r0, b_r0 = _router_weights(moe_w_group[0], moe_b_group[0], moe_w_expert[0], moe_b_expert[0])
    w_out0 = a_w_out[0].astype(BF16)
    x_mid0, route0, counts0 = out_proj(hm, mo0, w_out0[:a_v], w_out0[a_v:], xt, ffn_norm[0:1], w_r0, b_r0, tm)
    x1 = moe_layer(x_mid0, route0, counts0, ffn_norm[0:1], w_gate, w_up, w_down, 0)

    half = DIFF_HD // 2
    lane = jnp.arange(LANES)
    inv_freq = ROPE_THETA ** (-(lane % half).astype(F32) / half)
    ang = positions.astype(F32).reshape(t, 1) * inv_freq[None, :]
    cos = jnp.cos(ang)
    sin = jnp.sin(ang) * jnp.where((lane % DIFF_HD) < half, -1.0, 1.0)[None, :]
    tn = 512
    w1 = jnp.concatenate([kv_w, b_w_in[0]], axis=1).astype(BF16)
    n_tiles = w1.shape[1] // tn
    k_tiles = diff_qk // tn
    q_lo = kv_w.shape[1] // tn
    reps = LANES // DIFF_HD
    ew = jnp.zeros((n_tiles, 1, LANES), F32)
    ew = ew.at[0:k_tiles].set(jnp.tile(kv_knorm, reps)[None, None, :])
    q_scale = DIFF_HD ** -0.5 * math.log2(math.e)
    ew = ew.at[q_lo:q_lo + k_tiles].set(jnp.tile(b_qnorm[0], reps)[None, None, :] * q_scale)
    norms1 = jnp.stack([kv_norm, attn_norm[1]])
    (proj1,) = norm_proj(x1, norms1, w1, tm=tm_proj, tn=tn, norm_bounds=(q_lo,), rope=(cos, sin, ew),
                         rope_ranges=((0, k_tiles), (q_lo, q_lo + k_tiles)))

    lambda_init = 0.8 - 0.6 * math.exp(-0.3 * 1)
    oa = diff_attn(proj1, b_lambda[0], b_subln[0].reshape(DIFF_DV, 1), lambda_init, batch, seq,
                   _row_tile(seq, DIFF_TQ))
    mo1 = mem_attn(proj1, (kv_w.shape[1] + diff_qk) // mem_w, mkv, 1, mem_qnorm[1:2], mem_knorm[1:2], batch, seq,
                   slots, _row_tile(seq, 512))
    w_r1, b_r1 = _router_weights(moe_w_group[1], moe_b_group[1], moe_w_expert[1], moe_b_expert[1])
    w_out1 = b_w_out[0].astype(BF16)
    n_oa = DIFF_HEADS * DIFF_DV
    x_mid1, route1, counts1 = out_proj(oa, mo1, w_out1[:n_oa], w_out1[n_oa:], x1, ffn_norm[1:2], w_r1, b_r1, tm)
    out = moe_layer(x_mid1, route1, counts1, ffn_norm[1:2], w_gate, w_up, w_down, 1)
    return out.reshape(batch, seq, d)
```

```python
import functools
import math

import jax
import jax.numpy as jnp
from jax import lax
from jax.experimental import pallas as pl
from jax.experimental.pallas import tpu as pltpu

F32 = jnp.float32
BF16 = jnp.bfloat16
EPS = 1e-6
LANES = 128
VMEM_LIMIT_BYTES = 56 * 1024 * 1024

MLSTM_HEADS = 6
MLSTM_DQK = 128
MLSTM_DV = 256
MLSTM_CHUNK = 128
MEM_HEADS = 4
MEM_HEAD_DIM = 128
DIFF_HEADS = 12
DIFF_HD = 64
DIFF_DV = 128
ROPE_THETA = 10000.0
N_GROUPS = 4
EXPERTS_PER_GROUP = 8
N_EXPERTS = N_GROUPS * EXPERTS_PER_GROUP
TOP_K = 2
MOE_BLOCK = 256


def _params(*semantics):
    return pltpu.CompilerParams(dimension_semantics=semantics, vmem_limit_bytes=VMEM_LIMIT_BYTES)


def _norm_proj_kernel(*refs, norm_bounds, rope_ranges, has_aux, hd):
    it = iter(refs)
    x_ref = next(it)
    normw_ref = next(it)
    w_ref = next(it)
    if rope_ranges:
        cos_ref, sin_ref, ew_ref = next(it), next(it), next(it)
    if has_aux:
        waux_ref = next(it)
    out_ref = next(it)
    if has_aux:
        aux_ref = next(it)
    h_ref = next(it)

    j = pl.program_id(1)
    n_norm = len(norm_bounds) + 1

    @pl.when(j == 0)
    def _():
        x = x_ref[...]
        xn = x * lax.rsqrt(jnp.mean(x * x, axis=-1, keepdims=True) + EPS)
        for n in range(n_norm):
            h_ref[n] = (xn * normw_ref[n:n + 1, :]).astype(BF16)
        if has_aux:
            aux_ref[...] = jnp.dot(h_ref[0], waux_ref[...], preferred_element_type=F32)

    sel = jnp.int32(0)
    for b in norm_bounds:
        sel = sel + (j >= b).astype(jnp.int32)
    def plain():
        out_ref[...] = jnp.dot(h_ref[sel], w_ref[...], preferred_element_type=F32).astype(out_ref.dtype)

    if not rope_ranges:
        plain()
        return

    is_rope = jnp.bool_(False)
    for lo, hi in rope_ranges:
        is_rope = is_rope | ((j >= lo) & (j < hi))

    @pl.when(is_rope)
    def _():
        tm, tn = out_ref.shape
        ew = ew_ref[j]
        cs = cos_ref[...]
        sn = sin_ref[...]
        r = lax.broadcasted_iota(jnp.int32, (2 * LANES, LANES), 0)
        c = lax.broadcasted_iota(jnp.int32, (2 * LANES, LANES), 1)
        seg = ((r % LANES) // hd == c // hd).astype(BF16)
        lane = lax.broadcasted_iota(jnp.int32, (tm, LANES), 1)
        first_half = (lane % hd) < (hd // 2)
        acc = jnp.dot(h_ref[sel], w_ref[...], preferred_element_type=F32)
        for s in range(tn // LANES):
            t = acc[:, s * LANES:(s + 1) * LANES]
            tt = t * t
            hi_part = tt.astype(BF16)
            lo_part = (tt - hi_part.astype(F32)).astype(BF16)
            ss = jnp.dot(jnp.concatenate([hi_part, lo_part], axis=1), seg, preferred_element_type=F32)
            y = t * lax.rsqrt(ss * (1.0 / hd) + EPS) * ew
            rot = jnp.where(first_half, pltpu.roll(y, LANES - hd // 2, 1), pltpu.roll(y, hd // 2, 1))
            out_ref[:, s * LANES:(s + 1) * LANES] = (y * cs + rot * sn).astype(out_ref.dtype)

    pl.when(jnp.logical_not(is_rope))(plain)


def norm_proj(x, norm_w, w, *, tm, tn, norm_bounds=(), rope=None, rope_ranges=(), w_aux=None):
    n_rows, d = x.shape
    n = w.shape[1]
    n_norm = norm_w.shape[0]
    grid = (n_rows // tm, n // tn)
    in_specs = [pl.BlockSpec((tm, d), lambda i, j: (i, 0)), pl.BlockSpec((n_norm, d), lambda i, j: (0, 0)),
                pl.BlockSpec((d, tn), lambda i, j: (0, j))]
    args = [x, norm_w, w]
    if rope_ranges:
        cos, sin, ew = rope
        in_specs += [pl.BlockSpec((tm, LANES), lambda i, j: (i, 0)), pl.BlockSpec((tm, LANES), lambda i, j: (i, 0)),
                     pl.BlockSpec(ew.shape, lambda i, j: (0, 0, 0))]
        args += [cos, sin, ew]
    if w_aux is not None:
        in_specs.append(pl.BlockSpec((d, LANES), lambda i, j: (0, 0)))
        args.append(w_aux)
    out_shape = [jax.ShapeDtypeStruct((n_rows, n), BF16)]
    out_specs = [pl.BlockSpec((tm, tn), lambda i, j: (i, j))]
    if w_aux is not None:
        out_shape.append(jax.ShapeDtypeStruct((n_rows, LANES), F32))
        out_specs.append(pl.BlockSpec((tm, LANES), lambda i, j: (i, 0)))
    kern = functools.partial(_norm_proj_kernel, norm_bounds=tuple(norm_bounds), rope_ranges=tuple(rope_ranges),
                             has_aux=w_aux is not None, hd=DIFF_HD)
    return pl.pallas_call(
        kern, grid=grid, in_specs=in_specs, out_specs=out_specs, out_shape=out_shape,
        scratch_shapes=[pltpu.VMEM((n_norm, tm, d), BF16)],
        compiler_params=_params("parallel", "arbitrary"), name="norm_proj")(*args)


def _log_sigmoid(x):
    return jnp.minimum(x, 0.0) - jnp.log1p(jnp.exp(-jnp.abs(x)))


def _mlstm_kernel(q_ref, k_ref, v_ref, o_ref, g_ref, bias_ref, hnorm_ref, out_ref, c_ref, n_ref, m_ref):
    L, H, DQK, DV = MLSTM_CHUNK, MLSTM_HEADS, MLSTM_DQK, MLSTM_DV
    scale = DQK ** -0.5

    @pl.when(pl.program_id(1) == 0)
    def _():
        c_ref[...] = jnp.zeros_like(c_ref)
        n_ref[...] = jnp.zeros_like(n_ref)
        m_ref[...] = jnp.zeros_like(m_ref)

    g = g_ref[...] + bias_ref[...]
    ls = _log_sigmoid(g)
    g_t = g.T
    row = lax.broadcasted_iota(jnp.int32, (L, L), 0)
    col = lax.broadcasted_iota(jnp.int32, (L, L), 1)
    causal = col <= row
    ls_hi = ls.astype(BF16)
    ls_mid = (ls - ls_hi.astype(F32)).astype(BF16)
    ls_lo = (ls - ls_hi.astype(F32) - ls_mid.astype(F32)).astype(BF16)
    tri = causal.astype(BF16)
    cum = jnp.dot(jnp.concatenate([tri, tri, tri], axis=1), jnp.concatenate([ls_hi, ls_mid, ls_lo], axis=0),
                  preferred_element_type=F32)
    cum_t = cum.T

    qs = [q_ref[:, h * DQK:(h + 1) * DQK] for h in range(H)]
    ks = [k_ref[:, h * DQK:(h + 1) * DQK] for h in range(H)]
    vs = [v_ref[:, h * DV:(h + 1) * DV] for h in range(H)]
    qk = [lax.dot_general(qs[h], ks[h], (((1,), (1,)), ((), ())), preferred_element_type=F32) for h in range(H)]
    qc = [jnp.dot(qs[h], c_ref[h].astype(BF16), preferred_element_type=F32) for h in range(H)]

    gate = []
    for h in range(H):
        i_col = g[:, h:h + 1]
        i_row = g_t[h:h + 1, :]
        b_col = cum[:, H + h:H + h + 1]
        b_row = cum_t[H + h:H + h + 1, :]
        m_prev = m_ref[h][:, 0:1]
        d_log = jnp.where(causal, b_col - b_row + i_row, -jnp.inf)
        inter = b_col + m_prev
        m_t = jnp.maximum(inter, jnp.max(d_log, axis=1, keepdims=True))
        w_intra = jnp.exp(d_log - m_t)
        w_inter = jnp.exp(inter - m_t) * scale
        b_last = b_row[:, L - 1:L]
        st_row = b_last - b_row + i_row
        m_new = jnp.maximum(b_last + m_prev, jnp.max(st_row, axis=1, keepdims=True))
        w_st_col = jnp.exp(b_last - b_col + i_col - m_new)
        decay = jnp.exp(b_last + m_prev - m_new)
        gate.append((m_t, w_intra, w_inter, m_new, w_st_col, decay))

    s_all = [qk[h] * scale * gate[h][1] for h in range(H)]
    sv = [jnp.dot(s_all[h].astype(BF16), vs[h], preferred_element_type=F32) for h in range(H)]
    kw = [ks[h].astype(F32) * gate[h][4] for h in range(H)]
    kv = [lax.dot_general(kw[h].astype(BF16), vs[h], (((0,), (0,)), ((), ())), preferred_element_type=F32)
          for h in range(H)]

    for h in range(H):
        m_t, _, w_inter, m_new, _, decay = gate[h]
        n_prev = n_ref[h]
        num = sv[h] + w_inter * qc[h]
        den = (jnp.sum(s_all[h], axis=1, keepdims=True)
               + w_inter * jnp.sum(qs[h].astype(F32) * n_prev, axis=1, keepdims=True))
        hout = num * (1.0 / jnp.maximum(jnp.abs(den), jnp.exp(-m_t)))
        hn = hout * lax.rsqrt(jnp.mean(hout * hout, axis=-1, keepdims=True) + EPS) * hnorm_ref[h:h + 1, :]
        og = o_ref[:, h * DV:(h + 1) * DV].astype(F32)
        out_ref[:, h * DV:(h + 1) * DV] = (hn * jax.nn.sigmoid(og)).astype(out_ref.dtype)
        c_ref[h] = decay * c_ref[h] + kv[h]
        n_ref[h] = decay * n_prev + jnp.sum(kw[h], axis=0, keepdims=True)
        m_ref[h] = jnp.broadcast_to(m_new, (1, LANES))


def mlstm(proj, gates, gate_bias, head_norm, batch, seq):
    L, H, DQK, DV = MLSTM_CHUNK, MLSTM_HEADS, MLSTM_DQK, MLSTM_DV
    nc = seq // L
    qk_w, v_w = H * DQK, H * DV
    row = lambda b, c: b * nc + c
    return pl.pallas_call(
        _mlstm_kernel, grid=(batch, nc),
        in_specs=[pl.BlockSpec((L, qk_w), lambda b, c: (row(b, c), 0)),
                  pl.BlockSpec((L, qk_w), lambda b, c: (row(b, c), 1)),
                  pl.BlockSpec((L, v_w), lambda b, c: (row(b, c), 1)),
                  pl.BlockSpec((L, v_w), lambda b, c: (row(b, c), 2)),
                  pl.BlockSpec((L, LANES), lambda b, c: (row(b, c), 0)),
                  pl.BlockSpec((1, LANES), lambda b, c: (0, 0)),
                  pl.BlockSpec((H, DV), lambda b, c: (0, 0))],
        out_specs=pl.BlockSpec((L, v_w), lambda b, c: (row(b, c), 0)),
        out_shape=jax.ShapeDtypeStruct((batch * seq, v_w), BF16),
        scratch_shapes=[pltpu.VMEM((H, DQK, DV), F32), pltpu.VMEM((H, 1, DQK), F32), pltpu.VMEM((H, 1, LANES), F32)],
        compiler_params=_params("parallel", "arbitrary"), name="mlstm")(
            proj, proj, proj, proj, gates, gate_bias, head_norm)


def _mem_attn_kernel(q_ref, k_ref, v_ref, qn_ref, kn_ref, out_ref):
    hd = MEM_HEAD_DIM
    scale = hd ** -0.5
    for h in range(MEM_HEADS):
        sl = slice(h * hd, (h + 1) * hd)
        q = q_ref[:, sl].astype(F32)
        q = q * lax.rsqrt(jnp.mean(q * q, axis=-1, keepdims=True) + EPS) * qn_ref[...]
        k = k_ref[:, sl].astype(F32)
        k = k * lax.rsqrt(jnp.mean(k * k, axis=-1, keepdims=True) + EPS) * kn_ref[...]
        s = lax.dot_general(q.astype(BF16), k.astype(BF16), (((1,), (1,)), ((), ())),
                            preferred_element_type=F32) * scale
        p = jnp.exp(s - jnp.max(s, axis=-1, keepdims=True))
        p = p / jnp.sum(p, axis=-1, keepdims=True)
        out_ref[:, sl] = jnp.dot(p.astype(BF16), v_ref[:, sl], preferred_element_type=F32).astype(out_ref.dtype)


def mem_attn(proj, q_col_block, mkv, layer, qnorm, knorm, batch, seq, slots, tm):
    width = MEM_HEADS * MEM_HEAD_DIM
    nt = seq // tm
    return pl.pallas_call(
        _mem_attn_kernel, grid=(batch, nt),
        in_specs=[pl.BlockSpec((tm, width), lambda b, i: (b * nt + i, q_col_block)),
                  pl.BlockSpec((slots, width), lambda b, i: (b, 2 * layer)),
                  pl.BlockSpec((slots, width), lambda b, i: (b, 2 * layer + 1)),
                  pl.BlockSpec((1, MEM_HEAD_DIM), lambda b, i: (0, 0)),
                  pl.BlockSpec((1, MEM_HEAD_DIM), lambda b, i: (0, 0))],
        out_specs=pl.BlockSpec((tm, width), lambda b, i: (b * nt + i, 0)),
        out_shape=jax.ShapeDtypeStruct((batch * seq, width), BF16),
        compiler_params=_params("parallel", "parallel"), name="mem_attn")(proj, mkv, mkv, qnorm, knorm)


def _out_proj_kernel(a_ref, m_ref, wa_ref, wm_ref, x_ref, fn_ref, wr_ref, br_ref, xo_ref, route_ref, cnt_ref,
                     base_ref):
    @pl.when(pl.program_id(0) == 0)
    def _():
        base_ref[...] = jnp.zeros_like(base_ref)

    y = jnp.dot(a_ref[...], wa_ref[...], preferred_element_type=F32)
    y = y + jnp.dot(m_ref[...], wm_ref[...], preferred_element_type=F32)
    x = x_ref[...] + y
    xo_ref[...] = x
    hn = (x * lax.rsqrt(jnp.mean(x * x, axis=-1, keepdims=True) + EPS) * fn_ref[...]).astype(BF16)
    lg = jnp.dot(hn, wr_ref[...], preferred_element_type=F32) + br_ref[...]
    lane = lax.broadcasted_iota(jnp.int32, lg.shape, 1).astype(F32)
    neg = -jnp.inf
    gl = jnp.where(lane < N_GROUPS, lg, neg)
    gmax = jnp.max(gl, axis=1, keepdims=True)
    g_w = 1.0 / jnp.sum(jnp.exp(gl - gmax), axis=1, keepdims=True)
    g_idx = jnp.min(jnp.where(gl == gmax, lane, float(LANES)), axis=1, keepdims=True)
    e_lo = N_GROUPS + EXPERTS_PER_GROUP * g_idx
    el = jnp.where((lane >= e_lo) & (lane < e_lo + EXPERTS_PER_GROUP), lg, neg)
    max1 = jnp.max(el, axis=1, keepdims=True)
    e1 = jnp.min(jnp.where(el == max1, lane, float(LANES)), axis=1, keepdims=True)
    el2 = jnp.where(lane == e1, neg, el)
    max2 = jnp.max(el2, axis=1, keepdims=True)
    e2 = jnp.min(jnp.where(el2 == max2, lane, float(LANES)), axis=1, keepdims=True)
    p2 = jnp.exp(max2 - max1)
    w1 = g_w / (1.0 + p2)
    w2 = g_w * p2 / (1.0 + p2)
    hot = ((lane == e1) | (lane == e2)).astype(BF16)
    tm = hot.shape[0]
    r = lax.broadcasted_iota(jnp.int32, (tm, tm), 0)
    c = lax.broadcasted_iota(jnp.int32, (tm, tm), 1)
    before = jnp.dot((c < r).astype(BF16), hot, preferred_element_type=F32) + base_ref[0:1, :]
    rank1 = jnp.sum(jnp.where(lane == e1, before, 0.0), axis=1, keepdims=True)
    rank2 = jnp.sum(jnp.where(lane == e2, before, 0.0), axis=1, keepdims=True)
    total = base_ref[0:1, :] + jnp.sum(hot.astype(F32), axis=0, keepdims=True)
    base_ref[...] = jnp.broadcast_to(total, base_ref.shape)
    cnt_ref[...] = jnp.broadcast_to(total, cnt_ref.shape)
    vals = (w1, w2, e1 - N_GROUPS, e2 - N_GROUPS, rank1, rank2)
    route = jnp.zeros_like(lg)
    for n, v in enumerate(vals):
        route = jnp.where(lane == n, v, route)
    route_ref[...] = route


def out_proj(a, m, w_a, w_m, x, ffn_norm, w_router, b_router, tm):
    t, d = x.shape
    ka, km = a.shape[1], m.shape[1]
    return pl.pallas_call(
        _out_proj_kernel, grid=(t // tm,),
        in_specs=[pl.BlockSpec((tm, ka), lambda i: (i, 0)), pl.BlockSpec((tm, km), lambda i: (i, 0)),
                  pl.BlockSpec((ka, d), lambda i: (0, 0)), pl.BlockSpec((km, d), lambda i: (0, 0)),
                  pl.BlockSpec((tm, d), lambda i: (i, 0)), pl.BlockSpec((1, d), lambda i: (0, 0)),
                  pl.BlockSpec((d, LANES), lambda i: (0, 0)), pl.BlockSpec((1, LANES), lambda i: (0, 0))],
        out_specs=[pl.BlockSpec((tm, d), lambda i: (i, 0)), pl.BlockSpec((tm, LANES), lambda i: (i, 0)),
                   pl.BlockSpec((8, LANES), lambda i: (0, 0))],
        out_shape=[jax.ShapeDtypeStruct((t, d), F32), jax.ShapeDtypeStruct((t, LANES), F32),
                   jax.ShapeDtypeStruct((8, LANES), F32)],
        scratch_shapes=[pltpu.VMEM((8, LANES), F32)],
        compiler_params=_params("arbitrary"), name="out_proj")(a, m, w_a, w_m, x, ffn_norm, w_router, b_router)


def _dispatch_kernel(dest_ref, pad_lo_ref, pad_n_ref, x_ref, xs_hbm, stage_ref, zero_ref, sem, zsem, *, tm, top_k):
    i = pl.program_id(0)
    n_rows = top_k * tm
    slot = i % 2

    def row_copy(step, slot, r):
        return pltpu.make_async_copy(stage_ref.at[slot, pl.ds(r % tm, 1)],
                                     xs_hbm.at[pl.ds(dest_ref[step * n_rows + r], 1)], sem.at[slot])

    def wait_step(slot):
        for _ in range(top_k):
            pltpu.make_async_copy(stage_ref.at[slot], xs_hbm.at[pl.ds(0, tm)], sem.at[slot]).wait()

    def zero_copy(e, r):
        return pltpu.make_async_copy(zero_ref.at[pl.ds(0, 1)], xs_hbm.at[pl.ds(pad_lo_ref[e] + r, 1)], zsem.at[0])

    def zero_tail(g):
        lo = pl.multiple_of(pad_lo_ref[N_EXPERTS] + g * zero_ref.shape[0], zero_ref.shape[0])
        return pltpu.make_async_copy(zero_ref, xs_hbm.at[pl.ds(lo, zero_ref.shape[0])], zsem.at[0])

    @pl.when(i == 0)
    def _():
        zero_ref[...] = jnp.zeros_like(zero_ref)
        tail_groups = pad_n_ref[N_EXPERTS] // zero_ref.shape[0]
        for e in range(N_EXPERTS):
            lax.fori_loop(0, pad_n_ref[e], lambda r, c, e=e: (zero_copy(e, r).start(), c)[1], 0)
        lax.fori_loop(0, tail_groups, lambda g, c: (zero_tail(g).start(), c)[1], 0)
        for e in range(N_EXPERTS):
            lax.fori_loop(0, pad_n_ref[e], lambda r, c, e=e: (zero_copy(e, r).wait(), c)[1], 0)
        lax.fori_loop(0, tail_groups, lambda g, c: (zero_tail(g).wait(), c)[1], 0)

    stage_ref[slot] = x_ref[...]
    for r in range(n_rows):
        row_copy(i, slot, r).start()

    @pl.when(i > 0)
    def _():
        wait_step(1 - slot)

    @pl.when(i == pl.num_programs(0) - 1)
    def _():
        wait_step(slot)


def moe_dispatch(x, dest, pad_lo, pad_n, n_sorted_rows, tm):
    t, d = x.shape
    kern = functools.partial(_dispatch_kernel, tm=tm, top_k=TOP_K)
    grid_spec = pltpu.PrefetchScalarGridSpec(
        num_scalar_prefetch=3, grid=(t // tm,),
        in_specs=[pl.BlockSpec((tm, d), lambda i, *_: (i, 0))], out_specs=pl.BlockSpec(memory_space=pl.ANY),
        scratch_shapes=[pltpu.VMEM((2, tm, d), F32), pltpu.VMEM((8, d), F32), pltpu.SemaphoreType.DMA((2,)),
                        pltpu.SemaphoreType.DMA((1,))])
    return pl.pallas_call(kern, grid_spec=grid_spec, out_shape=jax.ShapeDtypeStruct((n_sorted_rows, d), F32),
                          compiler_params=_params("arbitrary"), name="moe_dispatch")(dest, pad_lo, pad_n, x)


def _moe_kernel(blk_exp_ref, nused_ref, xs_ref, fn_ref, wg_ref, wu_ref, wd_ref, ys_ref):
    del blk_exp_ref

    @pl.when(pl.program_id(0) < nused_ref[0])
    def _():
        x = xs_ref[...]
        xn = (x * lax.rsqrt(jnp.mean(x * x, axis=-1, keepdims=True) + EPS) * fn_ref[...]).astype(BF16)
        gate = jnp.dot(xn, wg_ref[...], preferred_element_type=F32)
        up = jnp.dot(xn, wu_ref[...], preferred_element_type=F32)
        act = (gate * jax.nn.sigmoid(gate) * up).astype(BF16)
        ys_ref[...] = jnp.dot(act, wd_ref[...], preferred_element_type=F32)

    @pl.when(pl.program_id(0) >= nused_ref[0])
    def _():
        ys_ref[...] = jnp.zeros_like(ys_ref)


def moe_experts(xs, ffn_norm, blk_exp, n_used, w_gate, w_up, w_down, layer):
    bm = MOE_BLOCK
    p, d = xs.shape
    f = w_gate.shape[3]
    expert = lambda i, be, nu: (layer, be[i], 0, 0)
    grid_spec = pltpu.PrefetchScalarGridSpec(
        num_scalar_prefetch=2, grid=(p // bm,),
        in_specs=[pl.BlockSpec((bm, d), lambda i, be, nu: (jnp.minimum(i, nu[0] - 1), 0)),
                  pl.BlockSpec((1, d), lambda i, be, nu: (0, 0)),
                  pl.BlockSpec((None, None, d, f), expert),
                  pl.BlockSpec((None, None, d, f), expert),
                  pl.BlockSpec((None, None, f, d), expert)],
        out_specs=pl.BlockSpec((bm, d), lambda i, be, nu: (i, 0)))
    return pl.pallas_call(
        _moe_kernel, grid_spec=grid_spec, out_shape=jax.ShapeDtypeStruct((p, d), F32),
        compiler_params=_params("arbitrary"), name="moe_experts")(blk_exp, n_used, xs, ffn_norm, w_gate, w_up, w_down)


def _combine_kernel(dest_ref, x_ref, route_ref, ys_hbm, out_ref, ybuf, sem, *, tm, top_k):
    i = pl.program_id(0)
    n_rows = top_k * tm
    slot = i % 2

    def row_gather(step, slot, r):
        return pltpu.make_async_copy(ys_hbm.at[pl.ds(dest_ref[step * n_rows + r], 1)], ybuf.at[slot, pl.ds(r, 1)],
                                     sem.at[slot])

    def wait_gather(slot):
        pltpu.make_async_copy(ys_hbm.at[pl.ds(0, n_rows)], ybuf.at[slot], sem.at[slot]).wait()

    @pl.when(i == 0)
    def _():
        for r in range(n_rows):
            row_gather(0, 0, r).start()

    for r in range(n_rows):
        row_gather(i + 1, 1 - slot, r).start()
    wait_gather(slot)
    out = x_ref[...]
    for j in range(top_k):
        out = out + route_ref[:, j:j + 1] * ybuf[slot, j * tm:(j + 1) * tm, :]
    out_ref[...] = out

    @pl.when(i == pl.num_programs(0) - 1)
    def _():
        wait_gather(1 - slot)


def moe_combine(x, route, ys, dest, tm):
    t, d = x.shape
    kern = functools.partial(_combine_kernel, tm=tm, top_k=TOP_K)
    grid_spec = pltpu.PrefetchScalarGridSpec(
        num_scalar_prefetch=1, grid=(t // tm,),
        in_specs=[pl.BlockSpec((tm, d), lambda i, ds: (i, 0)), pl.BlockSpec((tm, LANES), lambda i, ds: (i, 0)),
                  pl.BlockSpec(memory_space=pl.ANY)],
        out_specs=pl.BlockSpec((tm, d), lambda i, ds: (i, 0)),
        scratch_shapes=[pltpu.VMEM((2, TOP_K * tm, d), F32), pltpu.SemaphoreType.DMA((2,))])
    return pl.pallas_call(kern, grid_spec=grid_spec, out_shape=jax.ShapeDtypeStruct((t, d), F32),
                          compiler_params=_params("arbitrary"), name="moe_combine")(dest, x, route, ys)


def moe_tables(route, counts, n_blocks, tm):
    bm = MOE_BLOCK
    t = route.shape[0]
    eid = route[:, 2:2 + TOP_K].astype(jnp.int32).T
    rank = route[:, 2 + TOP_K:2 + 2 * TOP_K].astype(jnp.int32).T
    cnt = counts[0, N_GROUPS:N_GROUPS + N_EXPERTS].astype(jnp.int32)
    padded = (cnt + bm - 1) // bm * bm
    pend = jnp.cumsum(padded)
    pstart = pend - padded
    seg = jnp.zeros_like(eid)
    for e in range(N_EXPERTS):
        seg = jnp.where(eid == e, pstart[e], seg)
    dest = (seg + rank).reshape(TOP_K, t // tm, tm).transpose(1, 0, 2).reshape(-1)
    dest = jnp.concatenate([dest, jnp.zeros((TOP_K * tm,), jnp.int32)])
    blk_start = jnp.arange(n_blocks, dtype=jnp.int32) * bm
    blk_exp = jnp.minimum(jnp.sum((pend[None, :] <= blk_start[:, None]).astype(jnp.int32), axis=1), N_EXPERTS - 1)
    n_used = (pend[-1] // bm).astype(jnp.int32).reshape(1)
    pad_lo = jnp.concatenate([pstart + cnt, pend[-1:]])
    pad_n = jnp.concatenate([padded - cnt, n_blocks * bm - pend[-1:]])
    return dest, blk_exp, n_used, pad_lo, pad_n


def moe_layer(x_mid, route, counts, ffn_norm, w_gate, w_up, w_down, layer):
    t = x_mid.shape[0]
    tm = _row_tile(t, MOE_BLOCK)
    n_blocks = -(-TOP_K * t // MOE_BLOCK) + N_EXPERTS
    dest, blk_exp, n_used, pad_lo, pad_n = moe_tables(route, counts, n_blocks, tm)
    xs = moe_dispatch(x_mid, dest, pad_lo, pad_n, n_blocks * MOE_BLOCK, tm)
    ys = moe_experts(xs, ffn_norm, blk_exp, n_used, w_gate, w_up, w_down, layer)
    return moe_combine(x_mid, route, ys, dest, tm)


DIFF_TQ = 1024
DIFF_KS = 256
DIFF_CW = 256
DIFF_AHEAD = 4
DIFF_ONES = 16


def _diff_attn_kernel(q_ref, k_ref, v_ref, lam_ref, subln_ref, out_ref, vt_ref, qst_ref, m_ref, acc_ref, s_ref,
                      *, tq, ks, cw, ahead, lambda_init):
    hd = DIFF_HD
    dv = DIFF_DV
    seq = k_ref.shape[0]
    qi = pl.program_id(2)
    n_cols = 2 * tq

    @pl.when(qi == 0)
    def _():
        for j in range(seq // tq):
            vt_ref[0:dv, j * tq:(j + 1) * tq] = v_ref[j * tq:(j + 1) * tq, :].astype(F32).T.astype(BF16)
        vt_ref[dv:dv + DIFF_ONES, :] = jnp.ones((DIFF_ONES, seq), BF16)

    qt = q_ref[...].astype(F32).T
    row = lax.broadcasted_iota(jnp.int32, qt.shape, 0)
    qst_ref[:, 0:tq] = jnp.where(row < hd, qt, 0.0).astype(BF16)
    qst_ref[:, tq:n_cols] = jnp.where(row >= hd, qt, 0.0).astype(BF16)
    m_ref[...] = jnp.full_like(m_ref, -jnp.inf)
    acc_ref[...] = jnp.zeros_like(acc_ref)

    assert ahead <= n_cols // cw and ks < tq, "the carried pieces are key piece 0 of the first column chunks"

    def scores(off, kc, c):
        k = k_ref[pl.ds(off + kc * ks, ks), :]
        return jnp.dot(k, qst_ref[:, c * cw:(c + 1) * cw], preferred_element_type=F32)

    for c in range(ahead):
        s_ref[c] = scores(0, 0, c)

    def tiles(first, n_tiles, last_diagonal):
        off0 = pl.multiple_of(first * tq, tq)
        items = []
        for tt in range(n_tiles):
            diagonal = last_diagonal and tt == n_tiles - 1
            for kc in range(tq // ks):
                for c in range(n_cols // cw):
                    q_lo = (c * cw) % tq
                    if diagonal and q_lo + cw <= kc * ks:
                        continue
                    items.append((tt, kc, c, diagonal and (kc + 1) * ks - 1 > q_lo))

        pending = {}
        for n, (tt, kc, c, masked) in enumerate(items):
            off = off0 + tt * tq
            s = s_ref[n] if n < ahead else pending.pop(n)
            nxt = n + ahead
            if nxt < len(items):
                pending[nxt] = scores(off0 + items[nxt][0] * tq, items[nxt][1], items[nxt][2])
            elif not last_diagonal:
                s_ref[nxt - len(items)] = scores(off0 + n_tiles * tq, 0, nxt - len(items))
            cols = slice(c * cw, (c + 1) * cw)
            if masked:
                kpos = kc * ks + lax.broadcasted_iota(jnp.int32, s.shape, 0)
                qpos = (c * cw) % tq + lax.broadcasted_iota(jnp.int32, s.shape, 1)
                s = jnp.where(kpos <= qpos, s, -jnp.inf)
            m_prev = m_ref[:, cols]
            m_new = jnp.maximum(m_prev, jnp.max(s, axis=0, keepdims=True))
            alpha = jnp.exp2(m_prev - m_new)
            p = jnp.exp2(s - m_new)
            vt = vt_ref[:, pl.ds(off + kc * ks, ks)]
            acc_ref[:, cols] = alpha * acc_ref[:, cols] + jnp.dot(vt, p.astype(BF16), preferred_element_type=F32)
            m_ref[:, cols] = m_new

    lax.fori_loop(0, qi, lambda ki, c: (tiles(ki, 1, False), c)[1], 0)
    tiles(qi, 1, True)

    lv = lam_ref[...]
    lam = (jnp.exp(jnp.sum(lv[0:1] * lv[1:2], axis=1, keepdims=True))
           - jnp.exp(jnp.sum(lv[2:3] * lv[3:4], axis=1, keepdims=True)) + lambda_init)
    o = (acc_ref[0:dv, 0:tq] / acc_ref[dv:dv + 1, 0:tq]
         - lam * (acc_ref[0:dv, tq:n_cols] / acc_ref[dv:dv + 1, tq:n_cols]))
    o = o * lax.rsqrt(jnp.mean(o * o, axis=0, keepdims=True) + EPS) * subln_ref[...] * (1.0 - lambda_init)
    out_ref[...] = o.T.astype(out_ref.dtype)


def diff_attn(proj, lam_vecs, subln_col, lambda_init, batch, seq, tq):
    nh = DIFF_HEADS
    nq = seq // tq
    kern = functools.partial(_diff_attn_kernel, tq=tq, ks=min(DIFF_KS, tq), cw=min(DIFF_CW, tq), ahead=DIFF_AHEAD,
                             lambda_init=lambda_init)
    return pl.pallas_call(
        kern, grid=(batch, nh, nq),
        in_specs=[pl.BlockSpec((tq, LANES), lambda b, h, i: (b * nq + i, 2 * nh + h)),
                  pl.BlockSpec((seq, LANES), lambda b, h, i: (b, h)),
                  pl.BlockSpec((seq, LANES), lambda b, h, i: (b, nh + h)),
                  pl.BlockSpec((4, DIFF_HD), lambda b, h, i: (0, 0)),
                  pl.BlockSpec((DIFF_DV, 1), lambda b, h, i: (0, 0))],
        out_specs=pl.BlockSpec((tq, DIFF_DV), lambda b, h, i: (b * nq + i, h)),
        out_shape=jax.ShapeDtypeStruct((batch * seq, nh * DIFF_DV), BF16),
        scratch_shapes=[pltpu.VMEM((DIFF_DV + DIFF_ONES, seq), BF16), pltpu.VMEM((LANES, 2 * tq), BF16),
                        pltpu.VMEM((1, 2 * tq), F32), pltpu.VMEM((DIFF_DV + DIFF_ONES, 2 * tq), F32),
                        pltpu.VMEM((DIFF_AHEAD, min(DIFF_KS, tq), min(DIFF_CW, tq)), F32)],
        compiler_params=_params("parallel", "parallel", "arbitrary"), name="diff_attn")(
            proj, proj, proj, lam_vecs, subln_col)


def _row_tile(n, want):
    while n % want:
        want //= 2
    return want


def _router_weights(w_group, b_group, w_expert, b_expert):
    d = w_group.shape[0]
    pad = LANES - N_GROUPS - N_EXPERTS
    w = jnp.concatenate([w_group, w_expert, jnp.zeros((d, pad), F32)], axis=1).astype(BF16)
    b = jnp.concatenate([b_group, b_expert, jnp.zeros((pad,), F32)]).reshape(1, LANES)
    return w, b


def kernel(x, mem, positions, attn_norm, mem_norm, mem_w_kv, mem_qnorm, mem_knorm, a_w_in, a_gate_bias,
           a_head_norm, a_w_out, kv_norm, kv_w, kv_knorm, b_w_in, b_qnorm, b_lambda, b_subln, b_w_out, ffn_norm,
           moe_w_group, moe_b_group, moe_w_expert, moe_b_expert, moe_w_gate, moe_w_up, moe_w_down):
    batch, seq, d = x.shape
    slots = mem.shape[1]
    t = batch * seq
    tm = _row_tile(t, 512)
    tm_proj = _row_tile(t, 1024)
    a_qk = MLSTM_HEADS * MLSTM_DQK
    a_v = MLSTM_HEADS * MLSTM_DV
    mem_w = MEM_HEADS * MEM_HEAD_DIM
    diff_qk = 2 * DIFF_HEADS * DIFF_HD
    xt = x.reshape(t, d)
    w_gate, w_up, w_down = moe_w_gate.astype(BF16), moe_w_up.astype(BF16), moe_w_down.astype(BF16)

    memt = mem.reshape(batch * slots, d)
    w_mkv = jnp.concatenate([mem_w_kv[0], mem_w_kv[1]], axis=1).astype(BF16)
    (mkv,) = norm_proj(memt, mem_norm, w_mkv, tm=_row_tile(batch * slots, 512), tn=512,
                       norm_bounds=(2 * mem_w // 512,))

    g_lo = 2 * a_qk + 2 * a_v
    w_in = a_w_in[0]
    w_main = jnp.concatenate([w_in[:, :g_lo], w_in[:, g_lo + 2 * MLSTM_HEADS:]], axis=1).astype(BF16)
    w_gates = jnp.pad(w_in[:, g_lo:g_lo + 2 * MLSTM_HEADS], ((0, 0), (0, LANES - 2 * MLSTM_HEADS))).astype(BF16)
    proj0, gates = norm_proj(xt, attn_norm[0:1], w_main, tm=tm_proj, tn=512, w_aux=w_gates)
    gate_bias = jnp.pad(a_gate_bias[0], (0, LANES - 2 * MLSTM_HEADS)).reshape(1, LANES)
    hm = mlstm(proj0, gates, gate_bias, a_head_norm[0], batch, seq)
    mo0 = mem_attn(proj0, g_lo // mem_w, mkv, 0, mem_qnorm[0:1], mem_knorm[0:1], batch, seq, slots, _row_tile(seq, 512))
    w_r0, b_r0 = _router_weights(moe_w_group[0], moe_b_group[0], moe_w_expert[0], moe_b_expert[0])
    w_out0 = a_w_out[0].astype(BF16)
    x_mid0, route0, counts0 = out_proj(hm, mo0, w_out0[:a_v], w_out0[a_v:], xt, ffn_norm[0:1], w_r0, b_r0, tm)
    x1 = moe_layer(x_mid0, route0, counts0, ffn_norm[0:1], w_gate, w_up, w_down, 0)

    half = DIFF_HD // 2
    lane = jnp.arange(LANES)
    inv_freq = ROPE_THETA ** (-(lane % half).astype(F32) / half)
    ang = positions.astype(F32).reshape(t, 1) * inv_freq[None, :]
    cos = jnp.cos(ang)
    sin = jnp.sin(ang) * jnp.where((lane % DIFF_HD) < half, -1.0, 1.0)[None, :]
    tn = 512
    w1 = jnp.concatenate([kv_w, b_w_in[0]], axis=1).astype(BF16)
    n_tiles = w1.shape[1] // tn
    k_tiles = diff_qk // tn
    q_lo = kv_w.shape[1] // tn
    reps = LANES // DIFF_HD
    ew = jnp.zeros((n_tiles, 1, LANES), F32)
    ew = ew.at[0:k_tiles].set(jnp.tile(kv_knorm, reps)[None, None, :])
    q_scale = DIFF_HD ** -0.5 * math.log2(math.e)
    ew = ew.at[q_lo:q_lo + k_tiles].set(jnp.tile(b_qnorm[0], reps)[None, None, :] * q_scale)
    norms1 = jnp.stack([kv_norm, attn_norm[1]])
    (proj1,) = norm_proj(x1, norms1, w1, tm=tm_proj, tn=tn, norm_bounds=(q_lo,), rope=(cos, sin, ew),
                         rope_ranges=((0, k_tiles), (q_lo, q_lo + k_tiles)))

    lambda_init = 0.8 - 0.6 * math.exp(-0.3 * 1)
    oa = diff_attn(proj1, b_lambda[0], b_subln[0].reshape(DIFF_DV, 1), lambda_init, batch, seq,
                   _row_tile(seq, DIFF_TQ))
    mo1 = mem_attn(proj1, (kv_w.shape[1] + diff_qk) // mem_w, mkv, 1, mem_qnorm[1:2], mem_knorm[1:2], batch, seq,
                   slots, _row_tile(seq, 512))
    w_r1, b_r1 = _router_weights(moe_w_group[1], moe_b_group[1], moe_w_expert[1], moe_b_expert[1])
    w_out1 = b_w_out[0].astype(BF16)
    n_oa = DIFF_HEADS * DIFF_DV
    x_mid1, route1, counts1 = out_proj(oa, mo1, w_out1[:n_oa], w_out1[n_oa:], x1, ffn_norm[1:2], w_r1, b_r1, tm)
    out = moe_layer(x_mid1, route1, counts1, ffn_norm[1:2], w_gate, w_up, w_down, 1)
    return out.reshape(batch, seq, d)
```

```python
import functools
import math

import jax
import jax.numpy as jnp
from jax import lax
from jax.experimental import pallas as pl
from jax.experimental.pallas import tpu as pltpu

F32 = jnp.float32
BF16 = jnp.bfloat16
EPS = 1e-6
LANES = 128
VMEM_LIMIT_BYTES = 56 * 1024 * 1024

MLSTM_HEADS = 6
MLSTM_DQK = 128
MLSTM_DV = 256
MLSTM_CHUNK = 128
MEM_HEADS = 4
MEM_HEAD_DIM = 128
DIFF_HEADS = 12
DIFF_HD = 64
DIFF_DV = 128
ROPE_THETA = 10000.0
N_GROUPS = 4
EXPERTS_PER_GROUP = 8
N_EXPERTS = N_GROUPS * EXPERTS_PER_GROUP
TOP_K = 2
MOE_BLOCK = 256


def _params(*semantics):
    return pltpu.CompilerParams(dimension_semantics=semantics, vmem_limit_bytes=VMEM_LIMIT_BYTES)


def _norm_proj_kernel(*refs, norm_bounds, rope_ranges, has_aux, hd):
    it = iter(refs)
    x_ref = next(it)
    normw_ref = next(it)
    w_ref = next(it)
    if rope_ranges:
        cos_ref, sin_ref, ew_ref = next(it), next(it), next(it)
    if has_aux:
        waux_ref = next(it)
    out_ref = next(it)
    if has_aux:
        aux_ref = next(it)
    h_ref = next(it)

    j = pl.program_id(1)
    n_norm = len(norm_bounds) + 1

    @pl.when(j == 0)
    def _():
        x = x_ref[...]
        xn = x * lax.rsqrt(jnp.mean(x * x, axis=-1, keepdims=True) + EPS)
        for n in range(n_norm):
            h_ref[n] = (xn * normw_ref[n:n + 1, :]).astype(BF16)
        if has_aux:
            aux_ref[...] = jnp.dot(h_ref[0], waux_ref[...], preferred_element_type=F32)

    sel = jnp.int32(0)
    for b in norm_bounds:
        sel = sel + (j >= b).astype(jnp.int32)
    def plain():
        out_ref[...] = jnp.dot(h_ref[sel], w_ref[...], preferred_element_type=F32).astype(out_ref.dtype)

    if not rope_ranges:
        plain()
        return

    is_rope = jnp.bool_(False)
    for lo, hi in rope_ranges:
        is_rope = is_rope | ((j >= lo) & (j < hi))

    @pl.when(is_rope)
    def _():
        tm, tn = out_ref.shape
        ew = ew_ref[j]
        cs = cos_ref[...]
        sn = sin_ref[...]
        r = lax.broadcasted_iota(jnp.int32, (2 * LANES, LANES), 0)
        c = lax.broadcasted_iota(jnp.int32, (2 * LANES, LANES), 1)
        seg = ((r % LANES) // hd == c // hd).astype(BF16)
        lane = lax.broadcasted_iota(jnp.int32, (tm, LANES), 1)
        first_half = (lane % hd) < (hd // 2)
        acc = jnp.dot(h_ref[sel], w_ref[...], preferred_element_type=F32)
        for s in range(tn // LANES):
            t = acc[:, s * LANES:(s + 1) * LANES]
            tt = t * t
            hi_part = tt.astype(BF16)
            lo_part = (tt - hi_part.astype(F32)).astype(BF16)
            ss = jnp.dot(jnp.concatenate([hi_part, lo_part], axis=1), seg, preferred_element_type=F32)
            y = t * lax.rsqrt(ss * (1.0 / hd) + EPS) * ew
            rot = jnp.where(first_half, pltpu.roll(y, LANES - hd // 2, 1), pltpu.roll(y, hd // 2, 1))
            out_ref[:, s * LANES:(s + 1) * LANES] = (y * cs + rot * sn).astype(out_ref.dtype)

    pl.when(jnp.logical_not(is_rope))(plain)


def norm_proj(x, norm_w, w, *, tm, tn, norm_bounds=(), rope=None, rope_ranges=(), w_aux=None):
    n_rows, d = x.shape
    n = w.shape[1]
    n_norm = norm_w.shape[0]
    grid = (n_rows // tm, n // tn)
    in_specs = [pl.BlockSpec((tm, d), lambda i, j: (i, 0)), pl.BlockSpec((n_norm, d), lambda i, j: (0, 0)),
                pl.BlockSpec((d, tn), lambda i, j: (0, j))]
    args = [x, norm_w, w]
    if rope_ranges:
        cos, sin, ew = rope
        in_specs += [pl.BlockSpec((tm, LANES), lambda i, j: (i, 0)), pl.BlockSpec((tm, LANES), lambda i, j: (i, 0)),
                     pl.BlockSpec(ew.shape, lambda i, j: (0, 0, 0))]
        args += [cos, sin, ew]
    if w_aux is not None:
        in_specs.append(pl.BlockSpec((d, LANES), lambda i, j: (0, 0)))
        args.append(w_aux)
    out_shape = [jax.ShapeDtypeStruct((n_rows, n), BF16)]
    out_specs = [pl.BlockSpec((tm, tn), lambda i, j: (i, j))]
    if w_aux is not None:
        out_shape.append(jax.ShapeDtypeStruct((n_rows, LANES), F32))
        out_specs.append(pl.BlockSpec((tm, LANES), lambda i, j: (i, 0)))
    kern = functools.partial(_norm_proj_kernel, norm_bounds=tuple(norm_bounds), rope_ranges=tuple(rope_ranges),
                             has_aux=w_aux is not None, hd=DIFF_HD)
    return pl.pallas_call(
        kern, grid=grid, in_specs=in_specs, out_specs=out_specs, out_shape=out_shape,
        scratch_shapes=[pltpu.VMEM((n_norm, tm, d), BF16)],
        compiler_params=_params("parallel", "arbitrary"), name="norm_proj")(*args)


def _log_sigmoid(x):
    return jnp.minimum(x, 0.0) - jnp.log1p(jnp.exp(-jnp.abs(x)))


def _mlstm_kernel(q_ref, k_ref, v_ref, o_ref, g_ref, bias_ref, hnorm_ref, out_ref, c_ref, n_ref, m_ref):
    L, H, DQK, DV = MLSTM_CHUNK, MLSTM_HEADS, MLSTM_DQK, MLSTM_DV
    scale = DQK ** -0.5

    @pl.when(pl.program_id(1) == 0)
    def _():
        c_ref[...] = jnp.zeros_like(c_ref)
        n_ref[...] = jnp.zeros_like(n_ref)
        m_ref[...] = jnp.zeros_like(m_ref)

    g = g_ref[...] + bias_ref[...]
    ls = _log_sigmoid(g)
    g_t = g.T
    row = lax.broadcasted_iota(jnp.int32, (L, L), 0)
    col = lax.broadcasted_iota(jnp.int32, (L, L), 1)
    causal = col <= row
    ls_hi = ls.astype(BF16)
    ls_mid = (ls - ls_hi.astype(F32)).astype(BF16)
    ls_lo = (ls - ls_hi.astype(F32) - ls_mid.astype(F32)).astype(BF16)
    tri = causal.astype(BF16)
    cum = jnp.dot(jnp.concatenate([tri, tri, tri], axis=1), jnp.concatenate([ls_hi, ls_mid, ls_lo], axis=0),
                  preferred_element_type=F32)
    cum_t = cum.T

    qs = [q_ref[:, h * DQK:(h + 1) * DQK] for h in range(H)]
    ks = [k_ref[:, h * DQK:(h + 1) * DQK] for h in range(H)]
    vs = [v_ref[:, h * DV:(h + 1) * DV] for h in range(H)]
    qk = [lax.dot_general(qs[h], ks[h], (((1,), (1,)), ((), ())), preferred_element_type=F32) for h in range(H)]
    qc = [jnp.dot(qs[h], c_ref[h].astype(BF16), preferred_element_type=F32) for h in range(H)]

    gate = []
    for h in range(H):
        i_col = g[:, h:h + 1]
        i_row = g_t[h:h + 1, :]
        b_col = cum[:, H + h:H + h + 1]
        b_row = cum_t[H + h:H + h + 1, :]
        m_prev = m_ref[h][:, 0:1]
        d_log = jnp.where(causal, b_col - b_row + i_row, -jnp.inf)
        inter = b_col + m_prev
        m_t = jnp.maximum(inter, jnp.max(d_log, axis=1, keepdims=True))
        w_intra = jnp.exp(d_log - m_t)
        w_inter = jnp.exp(inter - m_t) * scale
        b_last = b_row[:, L - 1:L]
        st_row = b_last - b_row + i_row
        m_new = jnp.maximum(b_last + m_prev, jnp.max(st_row, axis=1, keepdims=True))
        w_st_col = jnp.exp(b_last - b_col + i_col - m_new)
        decay = jnp.exp(b_last + m_prev - m_new)
        gate.append((m_t, w_intra, w_inter, m_new, w_st_col, decay))

    s_all = [qk[h] * scale * gate[h][1] for h in range(H)]
    sv = [jnp.dot(s_all[h].astype(BF16), vs[h], preferred_element_type=F32) for h in range(H)]
    kw = [ks[h].astype(F32) * gate[h][4] for h in range(H)]
    kv = [lax.dot_general(kw[h].astype(BF16), vs[h], (((0,), (0,)), ((), ())), preferred_element_type=F32)
          for h in range(H)]

    for h in range(H):
        m_t, _, w_inter, m_new, _, decay = gate[h]
        n_prev = n_ref[h]
        num = sv[h] + w_inter * qc[h]
        den = (jnp.sum(s_all[h], axis=1, keepdims=True)
               + w_inter * jnp.sum(qs[h].astype(F32) * n_prev, axis=1, keepdims=True))
        hout = num * (1.0 / jnp.maximum(jnp.abs(den), jnp.exp(-m_t)))
        hn = hout * lax.rsqrt(jnp.mean(hout * hout, axis=-1, keepdims=True) + EPS) * hnorm_ref[h:h + 1, :]
        og = o_ref[:, h * DV:(h + 1) * DV].astype(F32)
        out_ref[:, h * DV:(h + 1) * DV] = (hn * jax.nn.sigmoid(og)).astype(out_ref.dtype)
        c_ref[h] = decay * c_ref[h] + kv[h]
        n_ref[h] = decay * n_prev + jnp.sum(kw[h], axis=0, keepdims=True)
        m_ref[h] = jnp.broadcast_to(m_new, (1, LANES))


def mlstm(proj, gates, gate_bias, head_norm, batch, seq):
    L, H, DQK, DV = MLSTM_CHUNK, MLSTM_HEADS, MLSTM_DQK, MLSTM_DV
    nc = seq // L
    qk_w, v_w = H * DQK, H * DV
    row = lambda b, c: b * nc + c
    return pl.pallas_call(
        _mlstm_kernel, grid=(batch, nc),
        in_specs=[pl.BlockSpec((L, qk_w), lambda b, c: (row(b, c), 0)),
                  pl.BlockSpec((L, qk_w), lambda b, c: (row(b, c), 1)),
                  pl.BlockSpec((L, v_w), lambda b, c: (row(b, c), 1)),
                  pl.BlockSpec((L, v_w), lambda b, c: (row(b, c), 2)),
                  pl.BlockSpec((L, LANES), lambda b, c: (row(b, c), 0)),
                  pl.BlockSpec((1, LANES), lambda b, c: (0, 0)),
                  pl.BlockSpec((H, DV), lambda b, c: (0, 0))],
        out_specs=pl.BlockSpec((L, v_w), lambda b, c: (row(b, c), 0)),
        out_shape=jax.ShapeDtypeStruct((batch * seq, v_w), BF16),
        scratch_shapes=[pltpu.VMEM((H, DQK, DV), F32), pltpu.VMEM((H, 1, DQK), F32), pltpu.VMEM((H, 1, LANES), F32)],
        compiler_params=_params("parallel", "arbitrary"), name="mlstm")(
            proj, proj, proj, proj, gates, gate_bias, head_norm)


def _mem_attn_kernel(q_ref, k_ref, v_ref, qn_ref, kn_ref, out_ref):
    hd = MEM_HEAD_DIM
    scale = hd ** -0.5
    for h in range(MEM_HEADS):
        sl = slice(h * hd, (h + 1) * hd)
        q = q_ref[:, sl].astype(F32)
        q = q * lax.rsqrt(jnp.mean(q * q, axis=-1, keepdims=True) + EPS) * qn_ref[...]
        k = k_ref[:, sl].astype(F32)
        k = k * lax.rsqrt(jnp.mean(k * k, axis=-1, keepdims=True) + EPS) * kn_ref[...]
        s = lax.dot_general(q.astype(BF16), k.astype(BF16), (((1,), (1,)), ((), ())),
                            preferred_element_type=F32) * scale
        p = jnp.exp(s - jnp.max(s, axis=-1, keepdims=True))
        p = p / jnp.sum(p, axis=-1, keepdims=True)
        out_ref[:, sl] = jnp.dot(p.astype(BF16), v_ref[:, sl], preferred_element_type=F32).astype(out_ref.dtype)


def mem_attn(proj, q_col_block, mkv, layer, qnorm, knorm, batch, seq, slots, tm):
    width = MEM_HEADS * MEM_HEAD_DIM
    nt = seq // tm
    return pl.pallas_call(
        _mem_attn_kernel, grid=(batch, nt),
        in_specs=[pl.BlockSpec((tm, width), lambda b, i: (b * nt + i, q_col_block)),
                  pl.BlockSpec((slots, width), lambda b, i: (b, 2 * layer)),
                  pl.BlockSpec((slots, width), lambda b, i: (b, 2 * layer + 1)),
                  pl.BlockSpec((1, MEM_HEAD_DIM), lambda b, i: (0, 0)),
                  pl.BlockSpec((1, MEM_HEAD_DIM), lambda b, i: (0, 0))],
        out_specs=pl.BlockSpec((tm, width), lambda b, i: (b * nt + i, 0)),
        out_shape=jax.ShapeDtypeStruct((batch * seq, width), BF16),
        compiler_params=_params("parallel", "parallel"), name="mem_attn")(proj, mkv, mkv, qnorm, knorm)


def _out_proj_kernel(a_ref, m_ref, wa_ref, wm_ref, x_ref, fn_ref, wr_ref, br_ref, xo_ref, route_ref, rt_ref,
                     cnt_ref, base_ref):
    @pl.when(pl.program_id(0) == 0)
    def _():
        base_ref[...] = jnp.zeros_like(base_ref)

    y = jnp.dot(a_ref[...], wa_ref[...], preferred_element_type=F32)
    y = y + jnp.dot(m_ref[...], wm_ref[...], preferred_element_type=F32)
    x = x_ref[...] + y
    xo_ref[...] = x
    hn = (x * lax.rsqrt(jnp.mean(x * x, axis=-1, keepdims=True) + EPS) * fn_ref[...]).astype(BF16)
    lg = jnp.dot(hn, wr_ref[...], preferred_element_type=F32) + br_ref[...]
    lane = lax.broadcasted_iota(jnp.int32, lg.shape, 1).astype(F32)
    neg = -jnp.inf
    gl = jnp.where(lane < N_GROUPS, lg, neg)
    gmax = jnp.max(gl, axis=1, keepdims=True)
    g_w = 1.0 / jnp.sum(jnp.exp(gl - gmax), axis=1, keepdims=True)
    g_idx = jnp.min(jnp.where(gl == gmax, lane, float(LANES)), axis=1, keepdims=True)
    e_lo = N_GROUPS + EXPERTS_PER_GROUP * g_idx
    el = jnp.where((lane >= e_lo) & (lane < e_lo + EXPERTS_PER_GROUP), lg, neg)
    max1 = jnp.max(el, axis=1, keepdims=True)
    e1 = jnp.min(jnp.where(el == max1, lane, float(LANES)), axis=1, keepdims=True)
    el2 = jnp.where(lane == e1, neg, el)
    max2 = jnp.max(el2, axis=1, keepdims=True)
    e2 = jnp.min(jnp.where(el2 == max2, lane, float(LANES)), axis=1, keepdims=True)
    p2 = jnp.exp(max2 - max1)
    w1 = g_w / (1.0 + p2)
    w2 = g_w * p2 / (1.0 + p2)
    hot = ((lane == e1) | (lane == e2)).astype(BF16)
    tm = hot.shape[0]
    r = lax.broadcasted_iota(jnp.int32, (tm, tm), 0)
    c = lax.broadcasted_iota(jnp.int32, (tm, tm), 1)
    before = jnp.dot((c < r).astype(BF16), hot, preferred_element_type=F32) + base_ref[0:1, :]
    rank1 = jnp.sum(jnp.where(lane == e1, before, 0.0), axis=1, keepdims=True)
    rank2 = jnp.sum(jnp.where(lane == e2, before, 0.0), axis=1, keepdims=True)
    total = base_ref[0:1, :] + jnp.sum(hot.astype(F32), axis=0, keepdims=True)
    base_ref[...] = jnp.broadcast_to(total, base_ref.shape)
    cnt_ref[...] = jnp.broadcast_to(total, cnt_ref.shape)
    vals = (w1, w2, e1 - N_GROUPS, e2 - N_GROUPS, rank1, rank2)
    route = jnp.zeros_like(lg)
    for n, v in enumerate(vals):
        route = jnp.where(lane == n, v, route)
    route_ref[...] = route
    rt_ref[...] = route.T[0:rt_ref.shape[0], :]


def out_proj(a, m, w_a, w_m, x, ffn_norm, w_router, b_router, tm):
    t, d = x.shape
    ka, km = a.shape[1], m.shape[1]
    return pl.pallas_call(
        _out_proj_kernel, grid=(t // tm,),
        in_specs=[pl.BlockSpec((tm, ka), lambda i: (i, 0)), pl.BlockSpec((tm, km), lambda i: (i, 0)),
                  pl.BlockSpec((ka, d), lambda i: (0, 0)), pl.BlockSpec((km, d), lambda i: (0, 0)),
                  pl.BlockSpec((tm, d), lambda i: (i, 0)), pl.BlockSpec((1, d), lambda i: (0, 0)),
                  pl.BlockSpec((d, LANES), lambda i: (0, 0)), pl.BlockSpec((1, LANES), lambda i: (0, 0))],
        out_specs=[pl.BlockSpec((tm, d), lambda i: (i, 0)), pl.BlockSpec((tm, LANES), lambda i: (i, 0)),
                   pl.BlockSpec((8, tm), lambda i: (0, i)), pl.BlockSpec((8, LANES), lambda i: (0, 0))],
        out_shape=[jax.ShapeDtypeStruct((t, d), F32), jax.ShapeDtypeStruct((t, LANES), F32),
                   jax.ShapeDtypeStruct((8, t), F32), jax.ShapeDtypeStruct((8, LANES), F32)],
        scratch_shapes=[pltpu.VMEM((8, LANES), F32)],
        compiler_params=_params("arbitrary"), name="out_proj")(a, m, w_a, w_m, x, ffn_norm, w_router, b_router)


def _dispatch_kernel(dest_ref, pad_lo_ref, pad_n_ref, x_ref, xs_hbm, stage_ref, zero_ref, sem, zsem, *, tm, top_k):
    i = pl.program_id(0)
    n_rows = top_k * tm
    slot = i % 2

    def row_copy(step, slot, r):
        return pltpu.make_async_copy(stage_ref.at[slot, pl.ds(r % tm, 1)],
                                     xs_hbm.at[pl.ds(dest_ref[step * n_rows + r], 1)], sem.at[slot])

    def wait_step(slot):
        for _ in range(top_k):
            pltpu.make_async_copy(stage_ref.at[slot], xs_hbm.at[pl.ds(0, tm)], sem.at[slot]).wait()

    def zero_copy(e, r):
        return pltpu.make_async_copy(zero_ref.at[pl.ds(0, 1)], xs_hbm.at[pl.ds(pad_lo_ref[e] + r, 1)], zsem.at[0])

    def zero_tail(g):
        lo = pl.multiple_of(pad_lo_ref[N_EXPERTS] + g * zero_ref.shape[0], zero_ref.shape[0])
        return pltpu.make_async_copy(zero_ref, xs_hbm.at[pl.ds(lo, zero_ref.shape[0])], zsem.at[0])

    @pl.when(i == 0)
    def _():
        zero_ref[...] = jnp.zeros_like(zero_ref)
        tail_groups = pad_n_ref[N_EXPERTS] // zero_ref.shape[0]
        for e in range(N_EXPERTS):
            lax.fori_loop(0, pad_n_ref[e], lambda r, c, e=e: (zero_copy(e, r).start(), c)[1], 0)
        lax.fori_loop(0, tail_groups, lambda g, c: (zero_tail(g).start(), c)[1], 0)
        for e in range(N_EXPERTS):
            lax.fori_loop(0, pad_n_ref[e], lambda r, c, e=e: (zero_copy(e, r).wait(), c)[1], 0)
        lax.fori_loop(0, tail_groups, lambda g, c: (zero_tail(g).wait(), c)[1], 0)

    stage_ref[slot] = x_ref[...]
    for r in range(n_rows):
        row_copy(i, slot, r).start()

    @pl.when(i > 0)
    def _():
        wait_step(1 - slot)

    @pl.when(i == pl.num_programs(0) - 1)
    def _():
        wait_step(slot)


def moe_dispatch(x, dest, pad_lo, pad_n, n_sorted_rows, tm):
    t, d = x.shape
    kern = functools.partial(_dispatch_kernel, tm=tm, top_k=TOP_K)
    grid_spec = pltpu.PrefetchScalarGridSpec(
        num_scalar_prefetch=3, grid=(t // tm,),
        in_specs=[pl.BlockSpec((tm, d), lambda i, *_: (i, 0))], out_specs=pl.BlockSpec(memory_space=pl.ANY),
        scratch_shapes=[pltpu.VMEM((2, tm, d), F32), pltpu.VMEM((8, d), F32), pltpu.SemaphoreType.DMA((2,)),
                        pltpu.SemaphoreType.DMA((1,))])
    return pl.pallas_call(kern, grid_spec=grid_spec, out_shape=jax.ShapeDtypeStruct((n_sorted_rows, d), F32),
                          compiler_params=_params("arbitrary"), name="moe_dispatch")(dest, pad_lo, pad_n, x)


def _moe_kernel(blk_exp_ref, nused_ref, xs_ref, fn_ref, wg_ref, wu_ref, wd_ref, ys_ref):
    del blk_exp_ref

    @pl.when(pl.program_id(0) < nused_ref[0])
    def _():
        x = xs_ref[...]
        xn = (x * lax.rsqrt(jnp.mean(x * x, axis=-1, keepdims=True) + EPS) * fn_ref[...]).astype(BF16)
        gate = jnp.dot(xn, wg_ref[...], preferred_element_type=F32)
        up = jnp.dot(xn, wu_ref[...], preferred_element_type=F32)
        act = (gate * jax.nn.sigmoid(gate) * up).astype(BF16)
        ys_ref[...] = jnp.dot(act, wd_ref[...], preferred_element_type=F32)

    @pl.when(pl.program_id(0) >= nused_ref[0])
    def _():
        ys_ref[...] = jnp.zeros_like(ys_ref)


def moe_experts(xs, ffn_norm, blk_exp, n_used, w_gate, w_up, w_down, layer):
    bm = MOE_BLOCK
    p, d = xs.shape
    f = w_gate.shape[3]
    expert = lambda i, be, nu: (layer, be[i], 0, 0)
    grid_spec = pltpu.PrefetchScalarGridSpec(
        num_scalar_prefetch=2, grid=(p // bm,),
        in_specs=[pl.BlockSpec((bm, d), lambda i, be, nu: (jnp.minimum(i, nu[0] - 1), 0)),
                  pl.BlockSpec((1, d), lambda i, be, nu: (0, 0)),
                  pl.BlockSpec((None, None, d, f), expert),
                  pl.BlockSpec((None, None, d, f), expert),
                  pl.BlockSpec((None, None, f, d), expert)],
        out_specs=pl.BlockSpec((bm, d), lambda i, be, nu: (i, 0)))
    return pl.pallas_call(
        _moe_kernel, grid_spec=grid_spec, out_shape=jax.ShapeDtypeStruct((p, d), F32),
        compiler_params=_params("arbitrary"), name="moe_experts")(blk_exp, n_used, xs, ffn_norm, w_gate, w_up, w_down)


def _combine_kernel(dest_ref, x_ref, route_ref, ys_hbm, out_ref, ybuf, sem, *, tm, top_k):
    i = pl.program_id(0)
    n_rows = top_k * tm
    slot = i % 2

    def row_gather(step, slot, r):
        return pltpu.make_async_copy(ys_hbm.at[pl.ds(dest_ref[step * n_rows + r], 1)], ybuf.at[slot, pl.ds(r, 1)],
                                     sem.at[slot])

    def wait_gather(slot):
        pltpu.make_async_copy(ys_hbm.at[pl.ds(0, n_rows)], ybuf.at[slot], sem.at[slot]).wait()

    @pl.when(i == 0)
    def _():
        for r in range(n_rows):
            row_gather(0, 0, r).start()

    for r in range(n_rows):
        row_gather(i + 1, 1 - slot, r).start()
    wait_gather(slot)
    out = x_ref[...]
    for j in range(top_k):
        out = out + route_ref[:, j:j + 1] * ybuf[slot, j * tm:(j + 1) * tm, :]
    out_ref[...] = out

    @pl.when(i == pl.num_programs(0) - 1)
    def _():
        wait_gather(1 - slot)


def moe_combine(x, route, ys, dest, tm):
    t, d = x.shape
    kern = functools.partial(_combine_kernel, tm=tm, top_k=TOP_K)
    grid_spec = pltpu.PrefetchScalarGridSpec(
        num_scalar_prefetch=1, grid=(t // tm,),
        in_specs=[pl.BlockSpec((tm, d), lambda i, ds: (i, 0)), pl.BlockSpec((tm, LANES), lambda i, ds: (i, 0)),
                  pl.BlockSpec(memory_space=pl.ANY)],
        out_specs=pl.BlockSpec((tm, d), lambda i, ds: (i, 0)),
        scratch_shapes=[pltpu.VMEM((2, TOP_K * tm, d), F32), pltpu.SemaphoreType.DMA((2,))])
    return pl.pallas_call(kern, grid_spec=grid_spec, out_shape=jax.ShapeDtypeStruct((t, d), F32),
                          compiler_params=_params("arbitrary"), name="moe_combine")(dest, x, route, ys)


def moe_tables(route_t, counts, n_blocks, tm):
    bm = MOE_BLOCK
    t = route_t.shape[1]
    eid = route_t[2:2 + TOP_K].astype(jnp.int32)
    rank = route_t[2 + TOP_K:2 + 2 * TOP_K].astype(jnp.int32)
    cnt = counts[0, N_GROUPS:N_GROUPS + N_EXPERTS].astype(jnp.int32)
    padded = (cnt + bm - 1) // bm * bm
    pend = jnp.cumsum(padded)
    pstart = pend - padded
    seg = jnp.zeros_like(eid)
    for e in range(N_EXPERTS):
        seg = jnp.where(eid == e, pstart[e], seg)
    dest = (seg + rank).reshape(TOP_K, t // tm, tm).transpose(1, 0, 2).reshape(-1)
    dest = jnp.concatenate([dest, jnp.zeros((TOP_K * tm,), jnp.int32)])
    blk_start = jnp.arange(n_blocks, dtype=jnp.int32) * bm
    blk_exp = jnp.minimum(jnp.sum((pend[None, :] <= blk_start[:, None]).astype(jnp.int32), axis=1), N_EXPERTS - 1)
    n_used = (pend[-1] // bm).astype(jnp.int32).reshape(1)
    pad_lo = jnp.concatenate([pstart + cnt, pend[-1:]])
    pad_n = jnp.concatenate([padded - cnt, n_blocks * bm - pend[-1:]])
    return dest, blk_exp, n_used, pad_lo, pad_n


def moe_layer(x_mid, route, route_t, counts, ffn_norm, w_gate, w_up, w_down, layer):
    t = x_mid.shape[0]
    tm = _row_tile(t, MOE_BLOCK)
    n_blocks = -(-TOP_K * t // MOE_BLOCK) + N_EXPERTS
    dest, blk_exp, n_used, pad_lo, pad_n = moe_tables(route_t, counts, n_blocks, tm)
    xs = moe_dispatch(x_mid, dest, pad_lo, pad_n, n_blocks * MOE_BLOCK, tm)
    ys = moe_experts(xs, ffn_norm, blk_exp, n_used, w_gate, w_up, w_down, layer)
    return moe_combine(x_mid, route, ys, dest, tm)


DIFF_TQ = 1024
DIFF_KS = 256
DIFF_CW = 256
DIFF_AHEAD = 4
DIFF_ONES = 16


def _diff_attn_kernel(q_ref, k_ref, v_ref, lam_ref, subln_ref, out_ref, vt_ref, qst_ref, m_ref, acc_ref, s_ref,
                      *, tq, ks, cw, ahead, lambda_init):
    hd = DIFF_HD
    dv = DIFF_DV
    seq = k_ref.shape[0]
    qi = pl.program_id(2)
    n_cols = 2 * tq

    @pl.when(qi == 0)
    def _():
        for j in range(seq // tq):
            vt_ref[0:dv, j * tq:(j + 1) * tq] = v_ref[j * tq:(j + 1) * tq, :].astype(F32).T.astype(BF16)
        vt_ref[dv:dv + DIFF_ONES, :] = jnp.ones((DIFF_ONES, seq), BF16)

    qt = q_ref[...].astype(F32).T
    row = lax.broadcasted_iota(jnp.int32, qt.shape, 0)
    qst_ref[:, 0:tq] = jnp.where(row < hd, qt, 0.0).astype(BF16)
    qst_ref[:, tq:n_cols] = jnp.where(row >= hd, qt, 0.0).astype(BF16)
    m_ref[...] = jnp.full_like(m_ref, -jnp.inf)
    acc_ref[...] = jnp.zeros_like(acc_ref)

    assert ahead <= n_cols // cw and ks < tq, "the carried pieces are key piece 0 of the first column chunks"

    def scores(off, kc, c):
        k = k_ref[pl.ds(off + kc * ks, ks), :]
        return jnp.dot(k, qst_ref[:, c * cw:(c + 1) * cw], preferred_element_type=F32)

    for c in range(ahead):
        s_ref[c] = scores(0, 0, c)

    def tiles(first, n_tiles, last_diagonal):
        off0 = pl.multiple_of(first * tq, tq)
        items = []
        for tt in range(n_tiles):
            diagonal = last_diagonal and tt == n_tiles - 1
            for kc in range(tq // ks):
                for c in range(n_cols // cw):
                    q_lo = (c * cw) % tq
                    if diagonal and q_lo + cw <= kc * ks:
                        continue
                    items.append((tt, kc, c, diagonal and (kc + 1) * ks - 1 > q_lo))

        pending = {}
        for n, (tt, kc, c, masked) in enumerate(items):
            off = off0 + tt * tq
            s = s_ref[n] if n < ahead else pending.pop(n)
            nxt = n + ahead
            if nxt < len(items):
                pending[nxt] = scores(off0 + items[nxt][0] * tq, items[nxt][1], items[nxt][2])
            elif not last_diagonal:
                s_ref[nxt - len(items)] = scores(off0 + n_tiles * tq, 0, nxt - len(items))
            cols = slice(c * cw, (c + 1) * cw)
            if masked:
                kpos = kc * ks + lax.broadcasted_iota(jnp.int32, s.shape, 0)
                qpos = (c * cw) % tq + lax.broadcasted_iota(jnp.int32, s.shape, 1)
                s = jnp.where(kpos <= qpos, s, -jnp.inf)
            m_prev = m_ref[:, cols]
            m_new = jnp.maximum(m_prev, jnp.max(s, axis=0, keepdims=True))
            alpha = jnp.exp2(m_prev - m_new)
            p = jnp.exp2(s - m_new)
            vt = vt_ref[:, pl.ds(off + kc * ks, ks)]
            acc_ref[:, cols] = alpha * acc_ref[:, cols] + jnp.dot(vt, p.astype(BF16), preferred_element_type=F32)
            m_ref[:, cols] = m_new

    lax.fori_loop(0, qi, lambda ki, c: (tiles(ki, 1, False), c)[1], 0)
    tiles(qi, 1, True)

    lv = lam_ref[...]
    lam = (jnp.exp(jnp.sum(lv[0:1] * lv[1:2], axis=1, keepdims=True))
           - jnp.exp(jnp.sum(lv[2:3] * lv[3:4], axis=1, keepdims=True)) + lambda_init)
    o = (acc_ref[0:dv, 0:tq] / acc_ref[dv:dv + 1, 0:tq]
         - lam * (acc_ref[0:dv, tq:n_cols] / acc_ref[dv:dv + 1, tq:n_cols]))
    o = o * lax.rsqrt(jnp.mean(o * o, axis=0, keepdims=True) + EPS) * subln_ref[...] * (1.0 - lambda_init)
    out_ref[...] = o.T.astype(out_ref.dtype)


def diff_attn(proj, lam_vecs, subln_col, lambda_init, batch, seq, tq):
    nh = DIFF_HEADS
    nq = seq // tq
    kern = functools.partial(_diff_attn_kernel, tq=tq, ks=min(DIFF_KS, tq), cw=min(DIFF_CW, tq), ahead=DIFF_AHEAD,
                             lambda_init=lambda_init)
    return pl.pallas_call(
        kern, grid=(batch, nh, nq),
        in_specs=[pl.BlockSpec((tq, LANES), lambda b, h, i: (b * nq + i, 2 * nh + h)),
                  pl.BlockSpec((seq, LANES), lambda b, h, i: (b, h)),
                  pl.BlockSpec((seq, LANES), lambda b, h, i: (b, nh + h)),
                  pl.BlockSpec((4, DIFF_HD), lambda b, h, i: (0, 0)),
                  pl.BlockSpec((DIFF_DV, 1), lambda b, h, i: (0, 0))],
        out_specs=pl.BlockSpec((tq, DIFF_DV), lambda b, h, i: (b * nq + i, h)),
        out_shape=jax.ShapeDtypeStruct((batch * seq, nh * DIFF_DV), BF16),
        scratch_shapes=[pltpu.VMEM((DIFF_DV + DIFF_ONES, seq), BF16), pltpu.VMEM((LANES, 2 * tq), BF16),
                        pltpu.VMEM((1, 2 * tq), F32), pltpu.VMEM((DIFF_DV + DIFF_ONES, 2 * tq), F32),
                        pltpu.VMEM((DIFF_AHEAD, min(DIFF_KS, tq), min(DIFF_CW, tq)), F32)],
        compiler_params=_params("parallel", "parallel", "arbitrary"), name="diff_attn")(
            proj, proj, proj, lam_vecs, subln_col)


def _row_tile(n, want):
    while n % want:
        want //= 2
    return want


def _router_weights(w_group, b_group, w_expert, b_expert):
    d = w_group.shape[0]
    pad = LANES - N_GROUPS - N_EXPERTS
    w = jnp.concatenate([w_group, w_expert, jnp.zeros((d, pad), F32)], axis=1).astype(BF16)
    b = jnp.concatenate([b_group, b_expert, jnp.zeros((pad,), F32)]).reshape(1, LANES)
    return w, b


def kernel(x, mem, positions, attn_norm, mem_norm, mem_w_kv, mem_qnorm, mem_knorm, a_w_in, a_gate_bias,
           a_head_norm, a_w_out, kv_norm, kv_w, kv_knorm, b_w_in, b_qnorm, b_lambda, b_subln, b_w_out, ffn_norm,
           moe_w_group, moe_b_group, moe_w_expert, moe_b_expert, moe_w_gate, moe_w_up, moe_w_down):
    batch, seq, d = x.shape
    slots = mem.shape[1]
    t = batch * seq
    tm = _row_tile(t, 512)
    tm_proj = _row_tile(t, 1024)
    a_qk = MLSTM_HEADS * MLSTM_DQK
    a_v = MLSTM_HEADS * MLSTM_DV
    mem_w = MEM_HEADS * MEM_HEAD_DIM
    diff_qk = 2 * DIFF_HEADS * DIFF_HD
    xt = x.reshape(t, d)
    w_gate, w_up, w_down = moe_w_gate.astype(BF16), moe_w_up.astype(BF16), moe_w_down.astype(BF16)

    memt = mem.reshape(batch * slots, d)
    w_mkv = jnp.concatenate([mem_w_kv[0], mem_w_kv[1]], axis=1).astype(BF16)
    (mkv,) = norm_proj(memt, mem_norm, w_mkv, tm=_row_tile(batch * slots, 512), tn=512,
                       norm_bounds=(2 * mem_w // 512,))

    g_lo = 2 * a_qk + 2 * a_v
    w_in = a_w_in[0]
    w_main = jnp.concatenate([w_in[:, :g_lo], w_in[:, g_lo + 2 * MLSTM_HEADS:]], axis=1).astype(BF16)
    w_gates = jnp.pad(w_in[:, g_lo:g_lo + 2 * MLSTM_HEADS], ((0, 0), (0, LANES - 2 * MLSTM_HEADS))).astype(BF16)
    proj0, gates = norm_proj(xt, attn_norm[0:1], w_main, tm=tm_proj, tn=512, w_aux=w_gates)
    gate_bias = jnp.pad(a_gate_bias[0], (0, LANES - 2 * MLSTM_HEADS)).reshape(1, LANES)
    hm = mlstm(proj0, gates, gate_bias, a_head_norm[0], batch, seq)
    mo0 = mem_attn(proj0, g_lo // mem_w, mkv, 0, mem_qnorm[0:1], mem_knorm[0:1], batch, seq, slots, _row_tile(seq, 512))
    w_r0, b_r0 = _router_weights(moe_w_group[0], moe_b_group[0], moe_w_expert[0], moe_b_expert[0])
    w_out0 = a_w_out[0].astype(BF16)
    x_mid0, *route0 = out_proj(hm, mo0, w_out0[:a_v], w_out0[a_v:], xt, ffn_norm[0:1], w_r0, b_r0, tm)
    x1 = moe_layer(x_mid0, *route0, ffn_norm[0:1], w_gate, w_up, w_down, 0)

    half = DIFF_HD // 2
    lane = jnp.arange(LANES)
    inv_freq = ROPE_THETA ** (-(lane % half).astype(F32) / half)
    ang = positions.astype(F32).reshape(t, 1) * inv_freq[None, :]
    cos = jnp.cos(ang)
    sin = jnp.sin(ang) * jnp.where((lane % DIFF_HD) < half, -1.0, 1.0)[None, :]
    tn = 512
    w1 = jnp.concatenate([kv_w, b_w_in[0]], axis=1).astype(BF16)
    n_tiles = w1.shape[1] // tn
    k_tiles = diff_qk // tn
    q_lo = kv_w.shape[1] // tn
    reps = LANES // DIFF_HD
    ew = jnp.zeros((n_tiles, 1, LANES), F32)
    ew = ew.at[0:k_tiles].set(jnp.tile(kv_knorm, reps)[None, None, :])
    q_scale = DIFF_HD ** -0.5 * math.log2(math.e)
    ew = ew.at[q_lo:q_lo + k_tiles].set(jnp.tile(b_qnorm[0], reps)[None, None, :] * q_scale)
    norms1 = jnp.stack([kv_norm, attn_norm[1]])
    (proj1,) = norm_proj(x1, norms1, w1, tm=tm_proj, tn=tn, norm_bounds=(q_lo,), rope=(cos, sin, ew),
                         rope_ranges=((0, k_tiles), (q_lo, q_lo + k_tiles)))

    lambda_init = 0.8 - 0.6 * math.exp(-0.3 * 1)
    oa = diff_attn(proj1, b_lambda[0], b_subln[0].reshape(DIFF_DV, 1), lambda_init, batch, seq,
                   _row_tile(seq, DIFF_TQ))
    mo1 = mem_attn(proj1, (kv_w.shape[1] + diff_qk) // mem_w, mkv, 1, mem_qnorm[1:2], mem_knorm[1:2], batch, seq,
                   slots, _row_tile(seq, 512))
    w_r1, b_r1 = _router_weights(moe_w_group[1], moe_b_group[1], moe_w_expert[1], moe_b_expert[1])
    w_out1 = b_w_out[0].astype(BF16)
    n_oa = DIFF_HEADS * DIFF_DV
    x_mid1, *route1 = out_proj(oa, mo1, w_out1[:n_oa], w_out1[n_oa:], x1, ffn_norm[1:2], w_r1, b_r1, tm)
    out = moe_layer(x_mid1, *route1, ffn_norm[1:2], w_gate, w_up, w_down, 1)
    return out.reshape(batch, seq, d)
```

```python
import functools
import math

import jax
import jax.numpy as jnp
from jax import lax
from jax.experimental import pallas as pl
from jax.experimental.pallas import tpu as pltpu

F32 = jnp.float32
BF16 = jnp.bfloat16
EPS = 1e-6
LANES = 128
VMEM_LIMIT_BYTES = 56 * 1024 * 1024

MLSTM_HEADS = 6
MLSTM_DQK = 128
MLSTM_DV = 256
MLSTM_CHUNK = 128
MEM_HEADS = 4
MEM_HEAD_DIM = 128
DIFF_HEADS = 12
DIFF_HD = 64
DIFF_DV = 128
ROPE_THETA = 10000.0
N_GROUPS = 4
EXPERTS_PER_GROUP = 8
N_EXPERTS = N_GROUPS * EXPERTS_PER_GROUP
TOP_K = 2
MOE_BLOCK = 256
ROW_TILE = 512
PROJ_ROW_TILE = 1024
PROJ_COL_TILE = 512


def _params(*semantics):
    return pltpu.CompilerParams(dimension_semantics=semantics, vmem_limit_bytes=VMEM_LIMIT_BYTES)


def _norm_proj_kernel(*refs, norm_bounds, rope_ranges, has_aux, hd):
    it = iter(refs)
    x_ref = next(it)
    normw_ref = next(it)
    w_ref = next(it)
    if rope_ranges:
        cos_ref, sin_ref, ew_ref = next(it), next(it), next(it)
    if has_aux:
        waux_ref = next(it)
    out_ref = next(it)
    if has_aux:
        aux_ref = next(it)
    h_ref = next(it)

    j = pl.program_id(1)
    n_norm = len(norm_bounds) + 1

    @pl.when(j == 0)
    def _():
        x = x_ref[...]
        xn = x * lax.rsqrt(jnp.mean(x * x, axis=-1, keepdims=True) + EPS)
        for n in range(n_norm):
            h_ref[n] = (xn * normw_ref[n:n + 1, :]).astype(BF16)
        if has_aux:
            aux_ref[...] = jnp.dot(h_ref[0], waux_ref[...], preferred_element_type=F32)

    sel = jnp.int32(0)
    for b in norm_bounds:
        sel = sel + (j >= b).astype(jnp.int32)
    def plain():
        out_ref[...] = jnp.dot(h_ref[sel], w_ref[...], preferred_element_type=F32).astype(out_ref.dtype)

    if not rope_ranges:
        plain()
        return

    is_rope = jnp.bool_(False)
    for lo, hi in rope_ranges:
        is_rope = is_rope | ((j >= lo) & (j < hi))

    @pl.when(is_rope)
    def _():
        tm, tn = out_ref.shape
        ew = ew_ref[j]
        cs = cos_ref[...]
        sn = sin_ref[...]
        r = lax.broadcasted_iota(jnp.int32, (2 * LANES, LANES), 0)
        c = lax.broadcasted_iota(jnp.int32, (2 * LANES, LANES), 1)
        seg = ((r % LANES) // hd == c // hd).astype(BF16)
        lane = lax.broadcasted_iota(jnp.int32, (tm, LANES), 1)
        first_half = (lane % hd) < (hd // 2)
        acc = jnp.dot(h_ref[sel], w_ref[...], preferred_element_type=F32)
        for s in range(tn // LANES):
            t = acc[:, s * LANES:(s + 1) * LANES]
            tt = t * t
            hi_part = tt.astype(BF16)
            lo_part = (tt - hi_part.astype(F32)).astype(BF16)
            ss = jnp.dot(jnp.concatenate([hi_part, lo_part], axis=1), seg, preferred_element_type=F32)
            y = t * lax.rsqrt(ss * (1.0 / hd) + EPS) * ew
            rot = jnp.where(first_half, pltpu.roll(y, LANES - hd // 2, 1), pltpu.roll(y, hd // 2, 1))
            out_ref[:, s * LANES:(s + 1) * LANES] = (y * cs + rot * sn).astype(out_ref.dtype)

    pl.when(jnp.logical_not(is_rope))(plain)


def norm_proj(x, norm_w, w, *, tm, tn, norm_bounds=(), rope=None, rope_ranges=(), w_aux=None):
    n_rows, d = x.shape
    n = w.shape[1]
    n_norm = norm_w.shape[0]
    grid = (n_rows // tm, n // tn)
    in_specs = [pl.BlockSpec((tm, d), lambda i, j: (i, 0)), pl.BlockSpec((n_norm, d), lambda i, j: (0, 0)),
                pl.BlockSpec((d, tn), lambda i, j: (0, j))]
    args = [x, norm_w, w]
    if rope_ranges:
        cos, sin, ew = rope
        in_specs += [pl.BlockSpec((tm, LANES), lambda i, j: (i, 0)), pl.BlockSpec((tm, LANES), lambda i, j: (i, 0)),
                     pl.BlockSpec(ew.shape, lambda i, j: (0, 0, 0))]
        args += [cos, sin, ew]
    if w_aux is not None:
        in_specs.append(pl.BlockSpec((d, LANES), lambda i, j: (0, 0)))
        args.append(w_aux)
    out_shape = [jax.ShapeDtypeStruct((n_rows, n), BF16)]
    out_specs = [pl.BlockSpec((tm, tn), lambda i, j: (i, j))]
    if w_aux is not None:
        out_shape.append(jax.ShapeDtypeStruct((n_rows, LANES), F32))
        out_specs.append(pl.BlockSpec((tm, LANES), lambda i, j: (i, 0)))
    kern = functools.partial(_norm_proj_kernel, norm_bounds=tuple(norm_bounds), rope_ranges=tuple(rope_ranges),
                             has_aux=w_aux is not None, hd=DIFF_HD)
    return pl.pallas_call(
        kern, grid=grid, in_specs=in_specs, out_specs=out_specs, out_shape=out_shape,
        scratch_shapes=[pltpu.VMEM((n_norm, tm, d), BF16)],
        compiler_params=_params("parallel", "arbitrary"), name="norm_proj")(*args)


def _log_sigmoid(x):
    return jnp.minimum(x, 0.0) - jnp.log1p(jnp.exp(-jnp.abs(x)))


def _mlstm_kernel(q_ref, k_ref, v_ref, o_ref, g_ref, bias_ref, hnorm_ref, out_ref, c_ref, n_ref, m_ref):
    L, H, DQK, DV = MLSTM_CHUNK, MLSTM_HEADS, MLSTM_DQK, MLSTM_DV
    scale = DQK ** -0.5

    @pl.when(pl.program_id(1) == 0)
    def _():
        c_ref[...] = jnp.zeros_like(c_ref)
        n_ref[...] = jnp.zeros_like(n_ref)
        m_ref[...] = jnp.zeros_like(m_ref)

    g = g_ref[...] + bias_ref[...]
    ls = _log_sigmoid(g)
    g_t = g.T
    row = lax.broadcasted_iota(jnp.int32, (L, L), 0)
    col = lax.broadcasted_iota(jnp.int32, (L, L), 1)
    causal = col <= row
    ls_hi = ls.astype(BF16)
    ls_mid = (ls - ls_hi.astype(F32)).astype(BF16)
    ls_lo = (ls - ls_hi.astype(F32) - ls_mid.astype(F32)).astype(BF16)
    tri = causal.astype(BF16)
    cum = jnp.dot(jnp.concatenate([tri, tri, tri], axis=1), jnp.concatenate([ls_hi, ls_mid, ls_lo], axis=0),
                  preferred_element_type=F32)
    cum_t = cum.T

    qs = [q_ref[:, h * DQK:(h + 1) * DQK] for h in range(H)]
    ks = [k_ref[:, h * DQK:(h + 1) * DQK] for h in range(H)]
    vs = [v_ref[:, h * DV:(h + 1) * DV] for h in range(H)]
    qk = [lax.dot_general(qs[h], ks[h], (((1,), (1,)), ((), ())), preferred_element_type=F32) for h in range(H)]
    qc = [jnp.dot(qs[h], c_ref[h].astype(BF16), preferred_element_type=F32) for h in range(H)]

    gate = []
    for h in range(H):
        i_col = g[:, h:h + 1]
        i_row = g_t[h:h + 1, :]
        b_col = cum[:, H + h:H + h + 1]
        b_row = cum_t[H + h:H + h + 1, :]
        m_prev = m_ref[h][:, 0:1]
        d_log = jnp.where(causal, b_col - b_row + i_row, -jnp.inf)
        inter = b_col + m_prev
        m_t = jnp.maximum(inter, jnp.max(d_log, axis=1, keepdims=True))
        w_intra = jnp.exp(d_log - m_t)
        w_inter = jnp.exp(inter - m_t) * scale
        b_last = b_row[:, L - 1:L]
        st_row = b_last - b_row + i_row
        m_new = jnp.maximum(b_last + m_prev, jnp.max(st_row, axis=1, keepdims=True))
        w_st_col = jnp.exp(b_last - b_col + i_col - m_new)
        decay = jnp.exp(b_last + m_prev - m_new)
        gate.append((m_t, w_intra, w_inter, m_new, w_st_col, decay))

    s_all = [qk[h] * scale * gate[h][1] for h in range(H)]
    sv = [jnp.dot(s_all[h].astype(BF16), vs[h], preferred_element_type=F32) for h in range(H)]
    kw = [ks[h].astype(F32) * gate[h][4] for h in range(H)]
    kv = [lax.dot_general(kw[h].astype(BF16), vs[h], (((0,), (0,)), ((), ())), preferred_element_type=F32)
          for h in range(H)]

    for h in range(H):
        m_t, _, w_inter, m_new, _, decay = gate[h]
        n_prev = n_ref[h]
        num = sv[h] + w_inter * qc[h]
        den = (jnp.sum(s_all[h], axis=1, keepdims=True)
               + w_inter * jnp.sum(qs[h].astype(F32) * n_prev, axis=1, keepdims=True))
        hout = num * (1.0 / jnp.maximum(jnp.abs(den), jnp.exp(-m_t)))
        hn = hout * lax.rsqrt(jnp.mean(hout * hout, axis=-1, keepdims=True) + EPS) * hnorm_ref[h:h + 1, :]
        og = o_ref[:, h * DV:(h + 1) * DV].astype(F32)
        out_ref[:, h * DV:(h + 1) * DV] = (hn * jax.nn.sigmoid(og)).astype(out_ref.dtype)
        c_ref[h] = decay * c_ref[h] + kv[h]
        n_ref[h] = decay * n_prev + jnp.sum(kw[h], axis=0, keepdims=True)
        m_ref[h] = jnp.broadcast_to(m_new, (1, LANES))


def mlstm(proj, gates, gate_bias, head_norm, batch, seq):
    L, H, DQK, DV = MLSTM_CHUNK, MLSTM_HEADS, MLSTM_DQK, MLSTM_DV
    nc = seq // L
    qk_w, v_w = H * DQK, H * DV
    row = lambda b, c: b * nc + c
    return pl.pallas_call(
        _mlstm_kernel, grid=(batch, nc),
        in_specs=[pl.BlockSpec((L, qk_w), lambda b, c: (row(b, c), 0)),
                  pl.BlockSpec((L, qk_w), lambda b, c: (row(b, c), 1)),
                  pl.BlockSpec((L, v_w), lambda b, c: (row(b, c), 1)),
                  pl.BlockSpec((L, v_w), lambda b, c: (row(b, c), 2)),
                  pl.BlockSpec((L, LANES), lambda b, c: (row(b, c), 0)),
                  pl.BlockSpec((1, LANES), lambda b, c: (0, 0)),
                  pl.BlockSpec((H, DV), lambda b, c: (0, 0))],
        out_specs=pl.BlockSpec((L, v_w), lambda b, c: (row(b, c), 0)),
        out_shape=jax.ShapeDtypeStruct((batch * seq, v_w), BF16),
        scratch_shapes=[pltpu.VMEM((H, DQK, DV), F32), pltpu.VMEM((H, 1, DQK), F32), pltpu.VMEM((H, 1, LANES), F32)],
        compiler_params=_params("parallel", "arbitrary"), name="mlstm")(
            proj, proj, proj, proj, gates, gate_bias, head_norm)


def _mem_attn_kernel(q_ref, k_ref, v_ref, qn_ref, kn_ref, out_ref):
    hd = MEM_HEAD_DIM
    scale = hd ** -0.5
    for h in range(MEM_HEADS):
        sl = slice(h * hd, (h + 1) * hd)
        q = q_ref[:, sl].astype(F32)
        q = q * lax.rsqrt(jnp.mean(q * q, axis=-1, keepdims=True) + EPS) * qn_ref[...]
        k = k_ref[:, sl].astype(F32)
        k = k * lax.rsqrt(jnp.mean(k * k, axis=-1, keepdims=True) + EPS) * kn_ref[...]
        s = lax.dot_general(q.astype(BF16), k.astype(BF16), (((1,), (1,)), ((), ())),
                            preferred_element_type=F32) * scale
        p = jnp.exp(s - jnp.max(s, axis=-1, keepdims=True))
        p = p / jnp.sum(p, axis=-1, keepdims=True)
        out_ref[:, sl] = jnp.dot(p.astype(BF16), v_ref[:, sl], preferred_element_type=F32).astype(out_ref.dtype)


def mem_attn(proj, q_col_block, mkv, layer, qnorm, knorm, batch, seq, slots, tm):
    width = MEM_HEADS * MEM_HEAD_DIM
    nt = seq // tm
    return pl.pallas_call(
        _mem_attn_kernel, grid=(batch, nt),
        in_specs=[pl.BlockSpec((tm, width), lambda b, i: (b * nt + i, q_col_block)),
                  pl.BlockSpec((slots, width), lambda b, i: (b, 2 * layer)),
                  pl.BlockSpec((slots, width), lambda b, i: (b, 2 * layer + 1)),
                  pl.BlockSpec((1, MEM_HEAD_DIM), lambda b, i: (0, 0)),
                  pl.BlockSpec((1, MEM_HEAD_DIM), lambda b, i: (0, 0))],
        out_specs=pl.BlockSpec((tm, width), lambda b, i: (b * nt + i, 0)),
        out_shape=jax.ShapeDtypeStruct((batch * seq, width), BF16),
        compiler_params=_params("parallel", "parallel"), name="mem_attn")(proj, mkv, mkv, qnorm, knorm)


def _out_proj_kernel(a_ref, m_ref, wa_ref, wm_ref, x_ref, fn_ref, wr_ref, br_ref, xo_ref, route_ref, rt_ref,
                     cnt_ref, base_ref):
    @pl.when(pl.program_id(0) == 0)
    def _():
        base_ref[...] = jnp.zeros_like(base_ref)

    y = jnp.dot(a_ref[...], wa_ref[...], preferred_element_type=F32)
    y = y + jnp.dot(m_ref[...], wm_ref[...], preferred_element_type=F32)
    x = x_ref[...] + y
    xo_ref[...] = x
    hn = (x * lax.rsqrt(jnp.mean(x * x, axis=-1, keepdims=True) + EPS) * fn_ref[...]).astype(BF16)
    lg = jnp.dot(hn, wr_ref[...], preferred_element_type=F32) + br_ref[...]
    lane = lax.broadcasted_iota(jnp.int32, lg.shape, 1).astype(F32)
    neg = -jnp.inf
    gl = jnp.where(lane < N_GROUPS, lg, neg)
    gmax = jnp.max(gl, axis=1, keepdims=True)
    g_w = 1.0 / jnp.sum(jnp.exp(gl - gmax), axis=1, keepdims=True)
    g_idx = jnp.min(jnp.where(gl == gmax, lane, float(LANES)), axis=1, keepdims=True)
    e_lo = N_GROUPS + EXPERTS_PER_GROUP * g_idx
    el = jnp.where((lane >= e_lo) & (lane < e_lo + EXPERTS_PER_GROUP), lg, neg)
    max1 = jnp.max(el, axis=1, keepdims=True)
    e1 = jnp.min(jnp.where(el == max1, lane, float(LANES)), axis=1, keepdims=True)
    el2 = jnp.where(lane == e1, neg, el)
    max2 = jnp.max(el2, axis=1, keepdims=True)
    e2 = jnp.min(jnp.where(el2 == max2, lane, float(LANES)), axis=1, keepdims=True)
    p2 = jnp.exp(max2 - max1)
    w1 = g_w / (1.0 + p2)
    w2 = g_w * p2 / (1.0 + p2)
    hot = ((lane == e1) | (lane == e2)).astype(BF16)
    tm = hot.shape[0]
    r = lax.broadcasted_iota(jnp.int32, (tm, tm), 0)
    c = lax.broadcasted_iota(jnp.int32, (tm, tm), 1)
    before = jnp.dot((c < r).astype(BF16), hot, preferred_element_type=F32) + base_ref[0:1, :]
    rank1 = jnp.sum(jnp.where(lane == e1, before, 0.0), axis=1, keepdims=True)
    rank2 = jnp.sum(jnp.where(lane == e2, before, 0.0), axis=1, keepdims=True)
    total = base_ref[0:1, :] + jnp.sum(hot.astype(F32), axis=0, keepdims=True)
    base_ref[...] = jnp.broadcast_to(total, base_ref.shape)
    cnt_ref[...] = jnp.broadcast_to(total, cnt_ref.shape)
    vals = (w1, w2, e1 - N_GROUPS, e2 - N_GROUPS, rank1, rank2)
    route = jnp.zeros_like(lg)
    for n, v in enumerate(vals):
        route = jnp.where(lane == n, v, route)
    route_ref[...] = route
    rt_ref[...] = route.T[0:rt_ref.shape[0], :]


def out_proj(a, m, w_a, w_m, x, ffn_norm, w_router, b_router, tm):
    t, d = x.shape
    ka, km = a.shape[1], m.shape[1]
    return pl.pallas_call(
        _out_proj_kernel, grid=(t // tm,),
        in_specs=[pl.BlockSpec((tm, ka), lambda i: (i, 0)), pl.BlockSpec((tm, km), lambda i: (i, 0)),
                  pl.BlockSpec((ka, d), lambda i: (0, 0)), pl.BlockSpec((km, d), lambda i: (0, 0)),
                  pl.BlockSpec((tm, d), lambda i: (i, 0)), pl.BlockSpec((1, d), lambda i: (0, 0)),
                  pl.BlockSpec((d, LANES), lambda i: (0, 0)), pl.BlockSpec((1, LANES), lambda i: (0, 0))],
        out_specs=[pl.BlockSpec((tm, d), lambda i: (i, 0)), pl.BlockSpec((tm, LANES), lambda i: (i, 0)),
                   pl.BlockSpec((8, tm), lambda i: (0, i)), pl.BlockSpec((8, LANES), lambda i: (0, 0))],
        out_shape=[jax.ShapeDtypeStruct((t, d), F32), jax.ShapeDtypeStruct((t, LANES), F32),
                   jax.ShapeDtypeStruct((8, t), F32), jax.ShapeDtypeStruct((8, LANES), F32)],
        scratch_shapes=[pltpu.VMEM((8, LANES), F32)],
        compiler_params=_params("arbitrary"), name="out_proj")(a, m, w_a, w_m, x, ffn_norm, w_router, b_router)


def _dispatch_kernel(dest_ref, pad_lo_ref, pad_n_ref, x_ref, xs_hbm, stage_ref, zero_ref, sem, zsem, *, tm, top_k):
    i = pl.program_id(0)
    n_rows = top_k * tm
    slot = i % 2

    def row_copy(step, slot, r):
        return pltpu.make_async_copy(stage_ref.at[slot, pl.ds(r % tm, 1)],
                                     xs_hbm.at[pl.ds(dest_ref[step * n_rows + r], 1)], sem.at[slot])

    def wait_step(slot):
        for _ in range(top_k):
            pltpu.make_async_copy(stage_ref.at[slot], xs_hbm.at[pl.ds(0, tm)], sem.at[slot]).wait()

    def zero_copy(e, r):
        return pltpu.make_async_copy(zero_ref.at[pl.ds(0, 1)], xs_hbm.at[pl.ds(pad_lo_ref[e] + r, 1)], zsem.at[0])

    def zero_tail(g):
        lo = pl.multiple_of(pad_lo_ref[N_EXPERTS] + g * zero_ref.shape[0], zero_ref.shape[0])
        return pltpu.make_async_copy(zero_ref, xs_hbm.at[pl.ds(lo, zero_ref.shape[0])], zsem.at[0])

    @pl.when(i == 0)
    def _():
        zero_ref[...] = jnp.zeros_like(zero_ref)
        tail_groups = pad_n_ref[N_EXPERTS] // zero_ref.shape[0]
        for e in range(N_EXPERTS):
            lax.fori_loop(0, pad_n_ref[e], lambda r, c, e=e: (zero_copy(e, r).start(), c)[1], 0)
        lax.fori_loop(0, tail_groups, lambda g, c: (zero_tail(g).start(), c)[1], 0)
        for e in range(N_EXPERTS):
            lax.fori_loop(0, pad_n_ref[e], lambda r, c, e=e: (zero_copy(e, r).wait(), c)[1], 0)
        lax.fori_loop(0, tail_groups, lambda g, c: (zero_tail(g).wait(), c)[1], 0)

    stage_ref[slot] = x_ref[...]
    for r in range(n_rows):
        row_copy(i, slot, r).start()

    @pl.when(i > 0)
    def _():
        wait_step(1 - slot)

    @pl.when(i == pl.num_programs(0) - 1)
    def _():
        wait_step(slot)


def moe_dispatch(x, dest, pad_lo, pad_n, n_sorted_rows, tm):
    t, d = x.shape
    kern = functools.partial(_dispatch_kernel, tm=tm, top_k=TOP_K)
    grid_spec = pltpu.PrefetchScalarGridSpec(
        num_scalar_prefetch=3, grid=(t // tm,),
        in_specs=[pl.BlockSpec((tm, d), lambda i, *_: (i, 0))], out_specs=pl.BlockSpec(memory_space=pl.ANY),
        scratch_shapes=[pltpu.VMEM((2, tm, d), F32), pltpu.VMEM((8, d), F32), pltpu.SemaphoreType.DMA((2,)),
                        pltpu.SemaphoreType.DMA((1,))])
    return pl.pallas_call(kern, grid_spec=grid_spec, out_shape=jax.ShapeDtypeStruct((n_sorted_rows, d), F32),
                          compiler_params=_params("arbitrary"), name="moe_dispatch")(dest, pad_lo, pad_n, x)


def _moe_kernel(blk_exp_ref, nused_ref, xs_ref, fn_ref, wg_ref, wu_ref, wd_ref, ys_ref):
    del blk_exp_ref

    @pl.when(pl.program_id(0) < nused_ref[0])
    def _():
        x = xs_ref[...]
        xn = (x * lax.rsqrt(jnp.mean(x * x, axis=-1, keepdims=True) + EPS) * fn_ref[...]).astype(BF16)
        gate = jnp.dot(xn, wg_ref[...], preferred_element_type=F32)
        up = jnp.dot(xn, wu_ref[...], preferred_element_type=F32)
        act = (gate * jax.nn.sigmoid(gate) * up).astype(BF16)
        ys_ref[...] = jnp.dot(act, wd_ref[...], preferred_element_type=F32)

    @pl.when(pl.program_id(0) >= nused_ref[0])
    def _():
        ys_ref[...] = jnp.zeros_like(ys_ref)


def moe_experts(xs, ffn_norm, blk_exp, n_used, w_gate, w_up, w_down, layer):
    bm = MOE_BLOCK
    p, d = xs.shape
    f = w_gate.shape[3]
    expert = lambda i, be, nu: (layer, be[i], 0, 0)
    grid_spec = pltpu.PrefetchScalarGridSpec(
        num_scalar_prefetch=2, grid=(p // bm,),
        in_specs=[pl.BlockSpec((bm, d), lambda i, be, nu: (jnp.minimum(i, nu[0] - 1), 0)),
                  pl.BlockSpec((1, d), lambda i, be, nu: (0, 0)),
                  pl.BlockSpec((None, None, d, f), expert),
                  pl.BlockSpec((None, None, d, f), expert),
                  pl.BlockSpec((None, None, f, d), expert)],
        out_specs=pl.BlockSpec((bm, d), lambda i, be, nu: (i, 0)))
    return pl.pallas_call(
        _moe_kernel, grid_spec=grid_spec, out_shape=jax.ShapeDtypeStruct((p, d), F32),
        compiler_params=_params("arbitrary"), name="moe_experts")(blk_exp, n_used, xs, ffn_norm, w_gate, w_up, w_down)


def _combine_kernel(dest_ref, x_ref, route_ref, ys_hbm, out_ref, ybuf, sem, *, tm, top_k):
    i = pl.program_id(0)
    n_rows = top_k * tm
    slot = i % 2

    def row_gather(step, slot, r):
        return pltpu.make_async_copy(ys_hbm.at[pl.ds(dest_ref[step * n_rows + r], 1)], ybuf.at[slot, pl.ds(r, 1)],
                                     sem.at[slot])

    def wait_gather(slot):
        pltpu.make_async_copy(ys_hbm.at[pl.ds(0, n_rows)], ybuf.at[slot], sem.at[slot]).wait()

    @pl.when(i == 0)
    def _():
        for r in range(n_rows):
            row_gather(0, 0, r).start()

    for r in range(n_rows):
        row_gather(i + 1, 1 - slot, r).start()
    wait_gather(slot)
    out = x_ref[...]
    for j in range(top_k):
        out = out + route_ref[:, j:j + 1] * ybuf[slot, j * tm:(j + 1) * tm, :]
    out_ref[...] = out

    @pl.when(i == pl.num_programs(0) - 1)
    def _():
        wait_gather(1 - slot)


def moe_combine(x, route, ys, dest, tm):
    t, d = x.shape
    kern = functools.partial(_combine_kernel, tm=tm, top_k=TOP_K)
    grid_spec = pltpu.PrefetchScalarGridSpec(
        num_scalar_prefetch=1, grid=(t // tm,),
        in_specs=[pl.BlockSpec((tm, d), lambda i, ds: (i, 0)), pl.BlockSpec((tm, LANES), lambda i, ds: (i, 0)),
                  pl.BlockSpec(memory_space=pl.ANY)],
        out_specs=pl.BlockSpec((tm, d), lambda i, ds: (i, 0)),
        scratch_shapes=[pltpu.VMEM((2, TOP_K * tm, d), F32), pltpu.SemaphoreType.DMA((2,))])
    return pl.pallas_call(kern, grid_spec=grid_spec, out_shape=jax.ShapeDtypeStruct((t, d), F32),
                          compiler_params=_params("arbitrary"), name="moe_combine")(dest, x, route, ys)


def moe_tables(route_t, counts, n_blocks, tm):
    bm = MOE_BLOCK
    t = route_t.shape[1]
    eid = route_t[2:2 + TOP_K].astype(jnp.int32)
    rank = route_t[2 + TOP_K:2 + 2 * TOP_K].astype(jnp.int32)
    cnt = counts[0, N_GROUPS:N_GROUPS + N_EXPERTS].astype(jnp.int32)
    padded = (cnt + bm - 1) // bm * bm
    pend = jnp.cumsum(padded)
    pstart = pend - padded
    seg = jnp.zeros_like(eid)
    for e in range(N_EXPERTS):
        seg = jnp.where(eid == e, pstart[e], seg)
    dest = (seg + rank).reshape(TOP_K, t // tm, tm).transpose(1, 0, 2).reshape(-1)
    dest = jnp.concatenate([dest, jnp.zeros((TOP_K * tm,), jnp.int32)])
    blk_start = jnp.arange(n_blocks, dtype=jnp.int32) * bm
    blk_exp = jnp.minimum(jnp.sum((pend[None, :] <= blk_start[:, None]).astype(jnp.int32), axis=1), N_EXPERTS - 1)
    n_used = (pend[-1] // bm).astype(jnp.int32).reshape(1)
    pad_lo = jnp.concatenate([pstart + cnt, pend[-1:]])
    pad_n = jnp.concatenate([padded - cnt, n_blocks * bm - pend[-1:]])
    return dest, blk_exp, n_used, pad_lo, pad_n


def moe_layer(x_mid, route, route_t, counts, ffn_norm, w_gate, w_up, w_down, layer):
    t = x_mid.shape[0]
    tm = _row_tile(t, MOE_BLOCK)
    n_blocks = -(-TOP_K * t // MOE_BLOCK) + N_EXPERTS
    dest, blk_exp, n_used, pad_lo, pad_n = moe_tables(route_t, counts, n_blocks, tm)
    xs = moe_dispatch(x_mid, dest, pad_lo, pad_n, n_blocks * MOE_BLOCK, tm)
    ys = moe_experts(xs, ffn_norm, blk_exp, n_used, w_gate, w_up, w_down, layer)
    return moe_combine(x_mid, route, ys, dest, tm)


DIFF_TQ = 1024
DIFF_KS = 256
DIFF_CW = 256
DIFF_AHEAD = 4
DIFF_ONES = 16


def _diff_attn_kernel(q_ref, k_ref, v_ref, lam_ref, subln_ref, out_ref, vt_ref, qst_ref, m_ref, acc_ref, s_ref,
                      *, tq, ks, cw, ahead, lambda_init):
    hd = DIFF_HD
    dv = DIFF_DV
    seq = k_ref.shape[0]
    qi = pl.program_id(2)
    n_cols = 2 * tq

    @pl.when(qi == 0)
    def _():
        for j in range(seq // tq):
            vt_ref[0:dv, j * tq:(j + 1) * tq] = v_ref[j * tq:(j + 1) * tq, :].astype(F32).T.astype(BF16)
        vt_ref[dv:dv + DIFF_ONES, :] = jnp.ones((DIFF_ONES, seq), BF16)

    qt = q_ref[...].astype(F32).T
    row = lax.broadcasted_iota(jnp.int32, qt.shape, 0)
    qst_ref[:, 0:tq] = jnp.where(row < hd, qt, 0.0).astype(BF16)
    qst_ref[:, tq:n_cols] = jnp.where(row >= hd, qt, 0.0).astype(BF16)
    m_ref[...] = jnp.full_like(m_ref, -jnp.inf)
    acc_ref[...] = jnp.zeros_like(acc_ref)

    assert ahead <= n_cols // cw and ks < tq, "the carried pieces are key piece 0 of the first column chunks"

    def scores(off, kc, c):
        k = k_ref[pl.ds(off + kc * ks, ks), :]
        return jnp.dot(k, qst_ref[:, c * cw:(c + 1) * cw], preferred_element_type=F32)

    for c in range(ahead):
        s_ref[c] = scores(0, 0, c)

    def tiles(first, n_tiles, last_diagonal):
        off0 = pl.multiple_of(first * tq, tq)
        items = []
        for tt in range(n_tiles):
            diagonal = last_diagonal and tt == n_tiles - 1
            for kc in range(tq // ks):
                for c in range(n_cols // cw):
                    q_lo = (c * cw) % tq
                    if diagonal and q_lo + cw <= kc * ks:
                        continue
                    items.append((tt, kc, c, diagonal and (kc + 1) * ks - 1 > q_lo))

        pending = {}
        for n, (tt, kc, c, masked) in enumerate(items):
            off = off0 + tt * tq
            s = s_ref[n] if n < ahead else pending.pop(n)
            nxt = n + ahead
            if nxt < len(items):
                pending[nxt] = scores(off0 + items[nxt][0] * tq, items[nxt][1], items[nxt][2])
            elif not last_diagonal:
                s_ref[nxt - len(items)] = scores(off0 + n_tiles * tq, 0, nxt - len(items))
            cols = slice(c * cw, (c + 1) * cw)
            if masked:
                kpos = kc * ks + lax.broadcasted_iota(jnp.int32, s.shape, 0)
                qpos = (c * cw) % tq + lax.broadcasted_iota(jnp.int32, s.shape, 1)
                s = jnp.where(kpos <= qpos, s, -jnp.inf)
            m_prev = m_ref[:, cols]
            m_new = jnp.maximum(m_prev, jnp.max(s, axis=0, keepdims=True))
            alpha = jnp.exp2(m_prev - m_new)
            p = jnp.exp2(s - m_new)
            vt = vt_ref[:, pl.ds(off + kc * ks, ks)]
            acc_ref[:, cols] = alpha * acc_ref[:, cols] + jnp.dot(vt, p.astype(BF16), preferred_element_type=F32)
            m_ref[:, cols] = m_new

    lax.fori_loop(0, qi, lambda ki, c: (tiles(ki, 1, False), c)[1], 0)
    tiles(qi, 1, True)

    lv = lam_ref[...]
    lam = (jnp.exp(jnp.sum(lv[0:1] * lv[1:2], axis=1, keepdims=True))
           - jnp.exp(jnp.sum(lv[2:3] * lv[3:4], axis=1, keepdims=True)) + lambda_init)
    o = (acc_ref[0:dv, 0:tq] / acc_ref[dv:dv + 1, 0:tq]
         - lam * (acc_ref[0:dv, tq:n_cols] / acc_ref[dv:dv + 1, tq:n_cols]))
    o = o * lax.rsqrt(jnp.mean(o * o, axis=0, keepdims=True) + EPS) * subln_ref[...] * (1.0 - lambda_init)
    out_ref[...] = o.T.astype(out_ref.dtype)


def diff_attn(proj, lam_vecs, subln_col, lambda_init, batch, seq, tq):
    nh = DIFF_HEADS
    nq = seq // tq
    kern = functools.partial(_diff_attn_kernel, tq=tq, ks=min(DIFF_KS, tq), cw=min(DIFF_CW, tq), ahead=DIFF_AHEAD,
                             lambda_init=lambda_init)
    return pl.pallas_call(
        kern, grid=(batch, nh, nq),
        in_specs=[pl.BlockSpec((tq, LANES), lambda b, h, i: (b * nq + i, 2 * nh + h)),
                  pl.BlockSpec((seq, LANES), lambda b, h, i: (b, h)),
                  pl.BlockSpec((seq, LANES), lambda b, h, i: (b, nh + h)),
                  pl.BlockSpec((4, DIFF_HD), lambda b, h, i: (0, 0)),
                  pl.BlockSpec((DIFF_DV, 1), lambda b, h, i: (0, 0))],
        out_specs=pl.BlockSpec((tq, DIFF_DV), lambda b, h, i: (b * nq + i, h)),
        out_shape=jax.ShapeDtypeStruct((batch * seq, nh * DIFF_DV), BF16),
        scratch_shapes=[pltpu.VMEM((DIFF_DV + DIFF_ONES, seq), BF16), pltpu.VMEM((LANES, 2 * tq), BF16),
                        pltpu.VMEM((1, 2 * tq), F32), pltpu.VMEM((DIFF_DV + DIFF_ONES, 2 * tq), F32),
                        pltpu.VMEM((DIFF_AHEAD, min(DIFF_KS, tq), min(DIFF_CW, tq)), F32)],
        compiler_params=_params("parallel", "parallel", "arbitrary"), name="diff_attn")(
            proj, proj, proj, lam_vecs, subln_col)


def _row_tile(n, want):
    while n % want:
        want //= 2
    return want


def _router_weights(w_group, b_group, w_expert, b_expert):
    d = w_group.shape[0]
    pad = LANES - N_GROUPS - N_EXPERTS
    w = jnp.concatenate([w_group, w_expert, jnp.zeros((d, pad), F32)], axis=1).astype(BF16)
    b = jnp.concatenate([b_group, b_expert, jnp.zeros((pad,), F32)]).reshape(1, LANES)
    return w, b


def kernel(x, mem, positions, attn_norm, mem_norm, mem_w_kv, mem_qnorm, mem_knorm, a_w_in, a_gate_bias,
           a_head_norm, a_w_out, kv_norm, kv_w, kv_knorm, b_w_in, b_qnorm, b_lambda, b_subln, b_w_out, ffn_norm,
           moe_w_group, moe_b_group, moe_w_expert, moe_b_expert, moe_w_gate, moe_w_up, moe_w_down):
    batch, seq, d = x.shape
    slots = mem.shape[1]
    t = batch * seq
    tm = _row_tile(t, ROW_TILE)
    tm_proj = _row_tile(t, PROJ_ROW_TILE)
    tm_seq = _row_tile(seq, ROW_TILE)
    tn = PROJ_COL_TILE
    a_qk = MLSTM_HEADS * MLSTM_DQK
    a_v = MLSTM_HEADS * MLSTM_DV
    mem_w = MEM_HEADS * MEM_HEAD_DIM
    diff_qk = 2 * DIFF_HEADS * DIFF_HD
    xt = x.reshape(t, d)
    w_gate, w_up, w_down = moe_w_gate.astype(BF16), moe_w_up.astype(BF16), moe_w_down.astype(BF16)

    memt = mem.reshape(batch * slots, d)
    w_mkv = jnp.concatenate([mem_w_kv[0], mem_w_kv[1]], axis=1).astype(BF16)
    (mkv,) = norm_proj(memt, mem_norm, w_mkv, tm=_row_tile(batch * slots, ROW_TILE), tn=tn,
                       norm_bounds=(2 * mem_w // tn,))

    g_lo = 2 * a_qk + 2 * a_v
    w_in = a_w_in[0]
    w_main = jnp.concatenate([w_in[:, :g_lo], w_in[:, g_lo + 2 * MLSTM_HEADS:]], axis=1).astype(BF16)
    w_gates = jnp.pad(w_in[:, g_lo:g_lo + 2 * MLSTM_HEADS], ((0, 0), (0, LANES - 2 * MLSTM_HEADS))).astype(BF16)
    proj0, gates = norm_proj(xt, attn_norm[0:1], w_main, tm=tm_proj, tn=tn, w_aux=w_gates)
    gate_bias = jnp.pad(a_gate_bias[0], (0, LANES - 2 * MLSTM_HEADS)).reshape(1, LANES)
    hm = mlstm(proj0, gates, gate_bias, a_head_norm[0], batch, seq)
    mo0 = mem_attn(proj0, g_lo // mem_w, mkv, 0, mem_qnorm[0:1], mem_knorm[0:1], batch, seq, slots, tm_seq)
    w_r0, b_r0 = _router_weights(moe_w_group[0], moe_b_group[0], moe_w_expert[0], moe_b_expert[0])
    w_out0 = a_w_out[0].astype(BF16)
    x_mid0, *route0 = out_proj(hm, mo0, w_out0[:a_v], w_out0[a_v:], xt, ffn_norm[0:1], w_r0, b_r0, tm)
    x1 = moe_layer(x_mid0, *route0, ffn_norm[0:1], w_gate, w_up, w_down, 0)

    half = DIFF_HD // 2
    lane = jnp.arange(LANES)
    inv_freq = ROPE_THETA ** (-(lane % half).astype(F32) / half)
    ang = positions.astype(F32).reshape(t, 1) * inv_freq[None, :]
    cos = jnp.cos(ang)
    sin = jnp.sin(ang) * jnp.where((lane % DIFF_HD) < half, -1.0, 1.0)[None, :]
    w1 = jnp.concatenate([kv_w, b_w_in[0]], axis=1).astype(BF16)
    n_tiles = w1.shape[1] // tn
    k_tiles = diff_qk // tn
    q_lo = kv_w.shape[1] // tn
    reps = LANES // DIFF_HD
    ew = jnp.zeros((n_tiles, 1, LANES), F32)
    ew = ew.at[0:k_tiles].set(jnp.tile(kv_knorm, reps)[None, None, :])
    q_scale = DIFF_HD ** -0.5 * math.log2(math.e)
    ew = ew.at[q_lo:q_lo + k_tiles].set(jnp.tile(b_qnorm[0], reps)[None, None, :] * q_scale)
    norms1 = jnp.stack([kv_norm, attn_norm[1]])
    (proj1,) = norm_proj(x1, norms1, w1, tm=tm_proj, tn=tn, norm_bounds=(q_lo,), rope=(cos, sin, ew),
                         rope_ranges=((0, k_tiles), (q_lo, q_lo + k_tiles)))

    lambda_init = 0.8 - 0.6 * math.exp(-0.3 * 1)
    oa = diff_attn(proj1, b_lambda[0], b_subln[0].reshape(DIFF_DV, 1), lambda_init, batch, seq,
                   _row_tile(seq, DIFF_TQ))
    mo1 = mem_attn(proj1, (kv_w.shape[1] + diff_qk) // mem_w, mkv, 1, mem_qnorm[1:2], mem_knorm[1:2], batch, seq,
                   slots, tm_seq)
    w_r1, b_r1 = _router_weights(moe_w_group[1], moe_b_group[1], moe_w_expert[1], moe_b_expert[1])
    w_out1 = b_w_out[0].astype(BF16)
    n_oa = DIFF_HEADS * DIFF_DV
    x_mid1, *route1 = out_proj(oa, mo1, w_out1[:n_oa], w_out1[n_oa:], x1, ffn_norm[1:2], w_r1, b_r1, tm)
    out = moe_layer(x_mid1, *route1, ffn_norm[1:2], w_gate, w_up, w_down, 1)
    return out.reshape(batch, seq, d)
```
